```python
import math
import jax
import jax.numpy as jnp
from jax import lax
import numpy as np

D_MODEL = 1024
BATCH = 2
SEQ = 16384
DEPTH = 1
DEC_BATCH = 128
DEC_SEQ = 8
PAST_LEN = 8192
PAGE_SIZE = 128

D_MIX = D_MODEL
ATT_HEADS = 4
DK = 64
DV = 2 * DK
ATT_DIM = ATT_HEADS * DV
CONV_DIM = D_MIX - ATT_DIM
CONV_GROUPS = 8
CONV_WIDTH = 3
N_EXPERTS = 32
TOP_K = 4
D_FF = D_MODEL
SWIGLU_ALPHA = 1.702
SWIGLU_LIMIT = 7.0
EXPERT_BLOCK = 128
Q_BLOCK = 128
RMS_EPS = 1e-6
N_ADA = 6
IN_COLS = 3 * ATT_DIM + 3 * CONV_DIM

kernel_name = "hymba_diffattn_shortconv_moe_adaln_step"


def rms_norm(x, g):
    xf = x.astype(jnp.float32)
    y = xf * lax.rsqrt(jnp.mean(xf * xf, axis=-1, keepdims=True) + RMS_EPS)
    return (y * g.astype(jnp.float32)).astype(x.dtype)


def adaln(c, w_ada, b_ada):
    m = jax.nn.silu(c) @ w_ada + b_ada
    return jnp.split(m[:, None, :], N_ADA, axis=-1)


def pre_mix(x, c, w_ada, b_ada, norm1_g, w_in, q_norm_g, k_norm_g):
    mod = adaln(c, w_ada, b_ada)
    h = rms_norm(x, norm1_g) * (1.0 + mod[1]) + mod[0]
    z = h @ w_in
    splits = [ATT_DIM, 2 * ATT_DIM, 3 * ATT_DIM, 3 * ATT_DIM + CONV_DIM, 3 * ATT_DIM + 2 * CONV_DIM]
    q, k, v, cx, cb, cc = jnp.split(z, splits, axis=-1)
    lead = x.shape[:2]
    q = rms_norm(q.reshape(*lead, ATT_HEADS, 2, DK), q_norm_g)
    k = rms_norm(k.reshape(*lead, ATT_HEADS, 2, DK), k_norm_g)
    v = v.reshape(*lead, ATT_HEADS, DV)
    u = cc * cx
    return mod, q, k, v, u, cb


def short_conv(u, prev, w):
    L = u.shape[1]
    up = jnp.concatenate([prev.astype(u.dtype), u], axis=1)
    y = w[0] * up[:, 0:L]
    for j in range(1, CONV_WIDTH):
        y = y + w[j] * up[:, j:j + L]
    return y, up[:, L:]


def diff_attn_prompt(q, k, v, lam):
    B, S = q.shape[:2]
    nb = S // Q_BLOCK
    scale = DK ** -0.5
    qb = jnp.moveaxis(q.reshape(B, nb, Q_BLOCK, ATT_HEADS, 2, DK), 1, 0)
    kf = k.astype(jnp.float32)
    vf = v.astype(jnp.float32)
    key_pos = jnp.arange(S)

    def one_block(args):
        q_blk, blk = args
        s = jnp.einsum('bqhcd,bkhcd->bhcqk', q_blk.astype(jnp.float32), kf) * scale
        q_pos = blk * Q_BLOCK + jnp.arange(Q_BLOCK)
        causal = key_pos[None, :] <= q_pos[:, None]
        p = jax.nn.softmax(jnp.where(causal, s, -jnp.inf), axis=-1)
        a = p[:, :, 0] - lam * p[:, :, 1]
        return jnp.einsum('bhqk,bkhe->bqhe', a, vf).astype(v.dtype)

    o = lax.map(one_block, (qb, jnp.arange(nb)))
    return jnp.moveaxis(o, 0, 1).reshape(B, S, ATT_HEADS, DV)


def diff_attn_sample(q, k_new, v_new, cache_k, cache_v, layer, page_table, lam):
    T = q.shape[1]
    n_past = page_table.shape[1] * PAGE_SIZE
    scale = DK ** -0.5
    key_pos = jnp.arange(n_past + T)
    q_pos = n_past + jnp.arange(T)
    causal = key_pos[None, :] <= q_pos[:, None]

    def one_seq(args):
        q_s, k_s, v_s, pages = args
        kp = cache_k[layer, pages].reshape(n_past, ATT_HEADS, 2, DK)
        vp = cache_v[layer, pages].reshape(n_past, ATT_HEADS, DV)
        keys = jnp.concatenate([kp, k_s.astype(kp.dtype)], axis=0).astype(jnp.float32)
        vals = jnp.concatenate([vp, v_s.astype(vp.dtype)], axis=0).astype(jnp.float32)
        s = jnp.einsum('qhcd,khcd->hcqk', q_s.astype(jnp.float32), keys) * scale
        p = jax.nn.softmax(jnp.where(causal, s, -jnp.inf), axis=-1)
        a = p[:, 0] - lam * p[:, 1]
        return jnp.einsum('hqk,khe->qhe', a, vals).astype(v_s.dtype)

    return lax.map(one_seq, (q, k_new, v_new, page_table))


def moe_ffn(h, w_router, b_router, w_gate, b_gate, w_up, b_up, w_down, b_down):
    lead = h.shape[:-1]
    xt = h.reshape(-1, D_MODEL)
    T = xt.shape[0]
    logits = xt.astype(jnp.float32) @ w_router.astype(jnp.float32) + b_router.astype(jnp.float32)
    top_v, top_e = lax.top_k(logits, TOP_K)
    top_p = jax.nn.softmax(top_v, axis=-1).astype(h.dtype)
    flat_e = top_e.reshape(-1)
    flat_t = jnp.arange(T * TOP_K, dtype=jnp.int32) // TOP_K
    flat_p = top_p.reshape(-1)
    order = jnp.argsort(flat_e)
    se = flat_e[order]
    counts = jnp.bincount(flat_e, length=N_EXPERTS)
    padded = (counts + EXPERT_BLOCK - 1) // EXPERT_BLOCK * EXPERT_BLOCK
    start = jnp.cumsum(counts) - counts
    pend = jnp.cumsum(padded)
    pstart = pend - padded
    dest = pstart[se] + jnp.arange(T * TOP_K) - start[se]
    n_rows = -(-(T * TOP_K) // EXPERT_BLOCK) * EXPERT_BLOCK + N_EXPERTS * EXPERT_BLOCK
    n_blocks = n_rows // EXPERT_BLOCK
    row_tok = jnp.full((n_rows,), T, jnp.int32).at[dest].set(flat_t[order])
    row_p = jnp.zeros((n_rows,), h.dtype).at[dest].set(flat_p[order])
    block_e = jnp.minimum(jnp.searchsorted(pend, jnp.arange(n_blocks) * EXPERT_BLOCK, side='right'),
                          N_EXPERTS - 1)
    x_pad = jnp.concatenate([xt, jnp.zeros((1, D_MODEL), xt.dtype)], axis=0)
    xb = x_pad[row_tok].reshape(n_blocks, EXPERT_BLOCK, D_MODEL)

    def expert_block(args):
        x_blk, e = args
        g = jnp.minimum(x_blk @ w_gate[e] + b_gate[e], SWIGLU_LIMIT)
        u = jnp.clip(x_blk @ w_up[e] + b_up[e], -SWIGLU_LIMIT, SWIGLU_LIMIT)
        a = g * jax.nn.sigmoid(SWIGLU_ALPHA * g) * (u + 1.0)
        return a @ w_down[e] + b_down[e]

    yb = lax.map(expert_block, (xb, block_e)).reshape(n_rows, D_MODEL)
    y = jnp.zeros((T + 1, D_MODEL), h.dtype).at[row_tok].add(yb * row_p[:, None])[:T]
    return y.reshape(*lead, D_MODEL)


def post_mix(x, mod, o_att, y_conv, lam_init, subln_g, w_out, norm2_g,
             w_router, b_router, w_gate, b_gate, w_up, b_up, w_down, b_down):
    lead = x.shape[:2]
    a = (rms_norm(o_att, subln_g) * (1.0 - lam_init)).reshape(*lead, ATT_DIM)
    x = x + mod[2] * (jnp.concatenate([a, y_conv], axis=-1) @ w_out)
    h = rms_norm(x, norm2_g) * (1.0 + mod[4]) + mod[3]
    return x + mod[5] * moe_ffn(h, w_router, b_router, w_gate, b_gate, w_up, b_up, w_down, b_down)


def setup_inputs(seed: int = 0) -> dict:
    key = jax.random.key(seed)
    ks = jax.random.split(key, 40)
    f32 = jnp.float32
    n_pages = PAST_LEN // PAGE_SIZE
    n_used = DEC_BATCH * n_pages
    n_pool = n_used + n_used // 4
    nrm = lambda k, shape, s: jax.random.normal(k, shape, f32) * s
    gain = lambda k, shape: 1.0 + 0.02 * jax.random.normal(k, shape, f32)
    page_table = jax.random.permutation(ks[5], n_pool)[:n_used].reshape(DEC_BATCH, n_pages).astype(jnp.int32)
    return {
        "x_prompt": nrm(ks[0], (BATCH, SEQ, D_MODEL), 1.0),
        "x_sample": nrm(ks[1], (DEC_BATCH, DEC_SEQ, D_MODEL), 1.0),
        "cache_k": nrm(ks[2], (DEPTH, n_pool, PAGE_SIZE, ATT_HEADS, DV), 1.0),
        "cache_v": nrm(ks[3], (DEPTH, n_pool, PAGE_SIZE, ATT_HEADS, DV), 1.0),
        "state_conv": nrm(ks[4], (DEPTH, DEC_BATCH, CONV_WIDTH - 1, CONV_DIM), 1.0),
        "page_table": page_table,
        "c_prompt": nrm(ks[6], (BATCH, D_MODEL), 1.0),
        "c_sample": nrm(ks[7], (DEC_BATCH, D_MODEL), 1.0),
        "w_ada": nrm(ks[8], (DEPTH, D_MODEL, N_ADA * D_MODEL), 0.5 * D_MODEL ** -0.5),
        "b_ada": nrm(ks[9], (DEPTH, N_ADA * D_MODEL), 0.01),
        "norm1_g": gain(ks[10], (DEPTH, D_MODEL)),
        "w_in": nrm(ks[11], (DEPTH, D_MODEL, IN_COLS), D_MODEL ** -0.5),
        "q_norm_g": gain(ks[12], (DEPTH, DK)),
        "k_norm_g": gain(ks[13], (DEPTH, DK)),
        "lambda_q1": nrm(ks[14], (DEPTH, DK), 0.1),
        "lambda_k1": nrm(ks[15], (DEPTH, DK), 0.1),
        "lambda_q2": nrm(ks[16], (DEPTH, DK), 0.1),
        "lambda_k2": nrm(ks[17], (DEPTH, DK), 0.1),
        "subln_g": gain(ks[18], (DEPTH, DV)),
        "conv_w": nrm(ks[19], (DEPTH, CONV_WIDTH, CONV_DIM), CONV_WIDTH ** -0.5),
        "w_out": nrm(ks[20], (DEPTH, D_MIX, D_MODEL), D_MIX ** -0.5),
        "norm2_g": gain(ks[21], (DEPTH, D_MODEL)),
        "w_router": nrm(ks[22], (DEPTH, D_MODEL, N_EXPERTS), D_MODEL ** -0.5),
        "b_router": nrm(ks[23], (DEPTH, N_EXPERTS), 0.01),
        "w_gate": nrm(ks[24], (DEPTH, N_EXPERTS, D_MODEL, D_FF), D_MODEL ** -0.5),
        "b_gate": nrm(ks[25], (DEPTH, N_EXPERTS, D_FF), 0.01),
        "w_up": nrm(ks[26], (DEPTH, N_EXPERTS, D_MODEL, D_FF), D_MODEL ** -0.5),
        "b_up": nrm(ks[27], (DEPTH, N_EXPERTS, D_FF), 0.01),
        "w_down": nrm(ks[28], (DEPTH, N_EXPERTS, D_FF, D_MODEL), D_FF ** -0.5),
        "b_down": nrm(ks[29], (DEPTH, N_EXPERTS, D_MODEL), 0.01),
    }


def reference(x_prompt, x_sample, cache_k, cache_v, state_conv, page_table, c_prompt, c_sample,
              w_ada, b_ada, norm1_g, w_in, q_norm_g, k_norm_g, lambda_q1, lambda_k1, lambda_q2,
              lambda_k2, subln_g, conv_w, w_out, norm2_g, w_router, b_router, w_gate, b_gate,
              w_up, b_up, w_down, b_down):
    f32 = jnp.float32
    yp, ys = x_prompt, x_sample
    kp_rows, vp_rows, cp_rows, ks_rows, vs_rows, cs_rows = [], [], [], [], [], []
    for l in range(DEPTH):
        lam_init = 0.8 - 0.6 * math.exp(-0.3 * l)
        lam = (jnp.exp(jnp.sum(lambda_q1[l].astype(f32) * lambda_k1[l].astype(f32)))
               - jnp.exp(jnp.sum(lambda_q2[l].astype(f32) * lambda_k2[l].astype(f32))) + lam_init)
        moe_w = (w_router[l], b_router[l], w_gate[l], b_gate[l], w_up[l], b_up[l], w_down[l], b_down[l])

        mod, q, k, v, u, gb = pre_mix(yp, c_prompt, w_ada[l], b_ada[l], norm1_g[l], w_in[l],
                                      q_norm_g[l], k_norm_g[l])
        o = diff_attn_prompt(q, k, v, lam)
        prev0 = jnp.zeros((yp.shape[0], CONV_WIDTH - 1, CONV_DIM), u.dtype)
        uc, conv_p = short_conv(u, prev0, conv_w[l])
        yp = post_mix(yp, mod, o, gb * uc, lam_init, subln_g[l], w_out[l], norm2_g[l], *moe_w)
        kp_rows.append(k.reshape(*k.shape[:2], ATT_HEADS, DV))
        vp_rows.append(v)
        cp_rows.append(conv_p)

        mod, q, k, v, u, gb = pre_mix(ys, c_sample, w_ada[l], b_ada[l], norm1_g[l], w_in[l],
                                      q_norm_g[l], k_norm_g[l])
        o = diff_attn_sample(q, k, v, cache_k, cache_v, l, page_table, lam)
        uc, conv_s = short_conv(u, state_conv[l], conv_w[l])
        ys = post_mix(ys, mod, o, gb * uc, lam_init, subln_g[l], w_out[l], norm2_g[l], *moe_w)
        ks_rows.append(k.reshape(*k.shape[:2], ATT_HEADS, DV))
        vs_rows.append(v)
        cs_rows.append(conv_s)

    k_prompt = jnp.stack(kp_rows, 0)
    v_prompt = jnp.stack(vp_rows, 0)
    conv_prompt = jnp.stack(cp_rows, 0)
    k_sample = jnp.stack(ks_rows, 0)
    v_sample = jnp.stack(vs_rows, 0)
    conv_sample = jnp.stack(cs_rows, 0)
    return (yp, ys, k_prompt, v_prompt, conv_prompt, k_sample, v_sample, conv_sample)
```

```python
import functools
import math

import jax
import jax.numpy as jnp
from jax import lax
from jax.experimental import pallas as pl
from jax.experimental.pallas import tpu as pltpu

F32 = jnp.float32
BF16 = jnp.bfloat16
I32 = jnp.int32

RMS_EPS = 1e-6
N_ADA = 6
TOP_K = 4
CONV_WIDTH = 3
SWIGLU_ALPHA = 1.702
SWIGLU_LIMIT = 7.0
LOG2E = 1.4426950408889634

LANES = 128
SUBLANES = 8
VMEM_LIMIT = 56 * 1024 * 1024

TOKEN_TILE = 512
ATTN_TILE = 512
EXPERT_BLOCK = 256
COMBINE_TILE = 256
PAGES_PER_STEP = 16


def _params(*sem):
    return pltpu.CompilerParams(dimension_semantics=sem, vmem_limit_bytes=VMEM_LIMIT)


def _split_bf16(a):
    hi = a.astype(BF16)
    lo = (a - hi.astype(F32)).astype(BF16)
    return hi, lo


def _dot3(a, w):
    a_hi, a_lo = _split_bf16(a)
    w_hi, w_lo = _split_bf16(w)
    d = functools.partial(jnp.dot, preferred_element_type=F32)
    return d(a_hi, w_hi) + (d(a_hi, w_lo) + d(a_lo, w_hi))


def _adaln_kernel(c_ref, w_ref, b_ref, o_ref):
    c = c_ref[...]
    s = c * jax.nn.sigmoid(c)
    o_ref[...] = _dot3(s, w_ref[...]) + b_ref[...]


def _adaln(c_all, w_ada, b_ada):
    rows, d = c_all.shape
    n = w_ada.shape[1]
    tn = d
    return pl.pallas_call(
        _adaln_kernel,
        grid=(n // tn,),
        in_specs=[pl.BlockSpec((rows, d), lambda j: (0, 0)),
                  pl.BlockSpec((d, tn), lambda j: (0, j)),
                  pl.BlockSpec((1, tn), lambda j: (0, j))],
        out_specs=pl.BlockSpec((rows, tn), lambda j: (0, j)),
        out_shape=jax.ShapeDtypeStruct((rows, n), F32),
        compiler_params=_params("parallel"),
        name="adaln",
    )(c_all, w_ada, b_ada.reshape(1, n))


def _premix_kernel(x_ref, shift_ref, scale_ref, g1_ref, win_ref, gq_ref, gk_ref, gmat_ref, cw_ref,
                   *rest, sample, tiles_per_batch, att, dk, q_scale):
    if sample:
        st_ref, q_out, kf_out, vf_out, yc_out, cs_out, ubuf = rest
    else:
        q_out, kf_out, vf_out, kb_out, vb_out, yc_out, cs_out, ubuf = rest
    tm = x_ref.shape[0]
    conv = yc_out.shape[1]

    x = x_ref[...]
    ms = jnp.mean(x * x, axis=-1, keepdims=True)
    xn = x * lax.rsqrt(ms + RMS_EPS) * g1_ref[...]
    h = (xn * (1.0 + scale_ref[...]) + shift_ref[...]).astype(BF16)
    z = jnp.dot(h, win_ref[...], preferred_element_type=F32)

    def group_norm(t, g_ref):
        ss = jnp.dot((t * t).astype(BF16), gmat_ref[...], preferred_element_type=F32)
        return t * lax.rsqrt(ss * (1.0 / dk) + RMS_EPS) * g_ref[...]

    q = group_norm(z[:, 0:att], gq_ref)
    k = group_norm(z[:, att:2 * att], gk_ref)
    v = z[:, 2 * att:3 * att]
    cx = z[:, 3 * att:3 * att + conv]
    cb = z[:, 3 * att + conv:3 * att + 2 * conv]
    cc = z[:, 3 * att + 2 * conv:3 * att + 3 * conv]

    q_out[...] = (q * q_scale).astype(q_out.dtype)
    kf_out[...] = k
    vf_out[...] = v
    if not sample:
        kb_out[...] = k.astype(BF16)
        vb_out[...] = v.astype(BF16)

    u = cc * cx
    w0 = cw_ref[0:1, :]
    w1 = cw_ref[1:2, :]
    w2 = cw_ref[2:3, :]
    if sample:
        nseq = tm // SUBLANES
        u3 = u.reshape(nseq, SUBLANES, conv)
        ubuf[:, 8:16, :] = u3
        ubuf[:, 6:8, :] = st_ref[...]
        um1 = ubuf[:, 7:15, :]
        um2 = ubuf[:, 6:14, :]
        y3 = w0 * um2 + w1 * um1 + w2 * u3
        y = y3.reshape(tm, conv)
        cs_out[...] = ubuf[:, 14:16, :]
    else:
        i = pl.program_id(0)

        @pl.when(i % tiles_per_batch == 0)
        def _():
            ubuf[0:8, :] = jnp.zeros((8, conv), F32)

        ubuf[8:tm + 8, :] = u
        um1 = ubuf[7:tm + 7, :]
        um2 = ubuf[6:tm + 6, :]
        y = w0 * um2 + w1 * um1 + w2 * u
        ubuf[0:8, :] = ubuf[tm:tm + 8, :]
        cs_out[...] = u[tm - 2:tm, :]
    yc_out[...] = (cb * y).astype(BF16)


def _premix(x, mod_specs, mods, g1, win_bf, gq, gk, gmat, conv_w, state, *, sample, seq_len, dk):
    t, d = x.shape
    att = gq.shape[1]
    conv = conv_w.shape[1]
    ncol = win_bf.shape[1]
    tm = min(TOKEN_TILE, t)
    nt = t // tm
    tiles_per_batch = max(seq_len // tm, 1)
    q_scale = dk ** -0.5 * LOG2E
    const = lambda shape: pl.BlockSpec(shape, lambda i: (0,) * len(shape))
    row = lambda w: pl.BlockSpec((tm, w), lambda i: (i, 0))
    in_specs = [row(d), mod_specs(0, tm), mod_specs(1, tm), const((1, d)), const((d, ncol)),
                const((1, att)), const((1, att)), const((att, att)), const((CONV_WIDTH, conv))]
    args = [x, mods, mods, g1, win_bf, gq, gk, gmat, conv_w]
    if sample:
        nseq = tm // SUBLANES
        in_specs.append(pl.BlockSpec((nseq, CONV_WIDTH - 1, conv), lambda i: (i, 0, 0)))
        args.append(state)
        out_specs = [row(att), row(att), row(att), row(conv),
                     pl.BlockSpec((nseq, CONV_WIDTH - 1, conv), lambda i: (i, 0, 0))]
        out_shape = [jax.ShapeDtypeStruct((t, att), F32), jax.ShapeDtypeStruct((t, att), F32),
                     jax.ShapeDtypeStruct((t, att), F32), jax.ShapeDtypeStruct((t, conv), BF16),
                     jax.ShapeDtypeStruct((t // SUBLANES, CONV_WIDTH - 1, conv), F32)]
        scratch = [pltpu.VMEM((nseq, 2 * SUBLANES, conv), F32)]
    else:
        nb = t // seq_len
        out_specs = [row(att), row(att), row(att), row(att), row(att), row(conv),
                     pl.BlockSpec((None, CONV_WIDTH - 1, conv), lambda i: (i // tiles_per_batch, 0, 0))]
        out_shape = [jax.ShapeDtypeStruct((t, att), BF16), jax.ShapeDtypeStruct((t, att), F32),
                     jax.ShapeDtypeStruct((t, att), F32), jax.ShapeDtypeStruct((t, att), BF16),
                     jax.ShapeDtypeStruct((t, att), BF16), jax.ShapeDtypeStruct((t, conv), BF16),
                     jax.ShapeDtypeStruct((nb, CONV_WIDTH - 1, conv), F32)]
        scratch = [pltpu.VMEM((tm + SUBLANES, conv), F32)]
    return pl.pallas_call(
        functools.partial(_premix_kernel, sample=sample, tiles_per_batch=tiles_per_batch,
                          att=att, dk=dk, q_scale=q_scale),
        grid=(nt,),
        in_specs=in_specs,
        out_specs=out_specs,
        out_shape=out_shape,
        scratch_shapes=scratch,
        compiler_params=_params("arbitrary"),
        name="premix_sample" if sample else "premix_prompt",
    )(*args)


def _lambda_value(lq1, lk1, lq2, lk2, lam_init):
    a = jnp.sum(lq1[...] * lk1[...], axis=-1, keepdims=True)
    b = jnp.sum(lq2[...] * lk2[...], axis=-1, keepdims=True)
    return jnp.exp(a) - jnp.exp(b) + lam_init


def _attn_kernel(qT_ref, k_ref, vT_ref, lq1, lk1, lq2, lk2, g_ref, o_ref,
                 acc1, acc2, m1, l1, m2, l2, *, dk, lam_init):
    tq = qT_ref.shape[1]
    qi = pl.program_id(2)
    qT = qT_ref[...].astype(F32)
    half = lax.broadcasted_iota(I32, qT.shape, 0) < dk
    rhs = (jnp.where(half, qT, 0.0).astype(BF16), jnp.where(half, 0.0, qT).astype(BF16))
    accs, ms, ls = (acc1, acc2), (m1, m2), (l1, l2)
    for hf in range(2):
        accs[hf][...] = jnp.zeros(accs[hf].shape, F32)
        ms[hf][...] = jnp.full(ms[hf].shape, -jnp.inf, F32)
        ls[hf][...] = jnp.zeros(ls[hf].shape, F32)

    def step(j, masked):
        kc = k_ref[pl.ds(pl.multiple_of(j * tq, tq), tq), :]
        vc = vT_ref[j]
        for hf in range(2):
            s = jnp.dot(kc, rhs[hf], preferred_element_type=F32)
            if masked:
                kpos = lax.broadcasted_iota(I32, s.shape, 0)
                qpos = lax.broadcasted_iota(I32, s.shape, 1)
                s = jnp.where(kpos <= qpos, s, -jnp.inf)
            m_prev = ms[hf][...]
            m_new = jnp.maximum(m_prev, jnp.max(s, axis=0, keepdims=True))
            alpha = jnp.exp2(m_prev - m_new)
            p = jnp.exp2(s - m_new)
            ls[hf][...] = alpha * ls[hf][...] + jnp.sum(p, axis=0, keepdims=True)
            accs[hf][...] = alpha * accs[hf][...] + jnp.dot(vc, p.astype(BF16),
                                                           preferred_element_type=F32)
            ms[hf][...] = m_new

    def body(j, c):
        step(j, False)
        return c

    lax.fori_loop(0, qi, body, 0)
    step(qi, True)

    lam = _lambda_value(lq1, lk1, lq2, lk2, lam_init)
    o = acc1[...] / l1[...] - lam * (acc2[...] / l2[...])
    on = o * lax.rsqrt(jnp.mean(o * o, axis=0, keepdims=True) + RMS_EPS) * (1.0 - lam_init)
    o_ref[...] = (on.T * g_ref[...]).astype(o_ref.dtype)


def _attn_prompt(qT, k_bf, vT, lams, subln_g, *, nheads, dk, lam_init):
    b, att, s = qT.shape
    dv = att // nheads
    tq = min(ATTN_TILE, s)
    nq = s // tq
    lam_spec = pl.BlockSpec((1, dk), lambda bi, hi, qi: (0, 0))
    return pl.pallas_call(
        functools.partial(_attn_kernel, dk=dk, lam_init=lam_init),
        grid=(b, nheads, nq),
        in_specs=[pl.BlockSpec((None, dv, tq), lambda bi, hi, qi: (bi, hi, qi)),
                  pl.BlockSpec((None, s, dv), lambda bi, hi, qi: (bi, 0, hi)),
                  pl.BlockSpec((None, nq, dv, tq), lambda bi, hi, qi: (bi, hi, 0, 0)),
                  lam_spec, lam_spec, lam_spec, lam_spec,
                  pl.BlockSpec((1, dv), lambda bi, hi, qi: (0, 0))],
        out_specs=pl.BlockSpec((None, tq, dv), lambda bi, hi, qi: (bi, qi, hi)),
        out_shape=jax.ShapeDtypeStruct((b, s, att), BF16),
        scratch_shapes=[pltpu.VMEM((dv, tq), F32), pltpu.VMEM((dv, tq), F32),
                        pltpu.VMEM((1, tq), F32), pltpu.VMEM((1, tq), F32),
                        pltpu.VMEM((1, tq), F32), pltpu.VMEM((1, tq), F32)],
        compiler_params=_params("parallel", "parallel", "arbitrary"),
        name="attn_prompt",
    )(qT, k_bf, vT, *lams, subln_g)


def _sattn_kernel(pt_ref, q_ref, kn_ref, vn_ref, *rest, pps, nheads, dk, page, lam_init):
    k_refs = rest[:pps]
    v_refs = rest[pps:2 * pps]
    lq1, lk1, lq2, lk2, g_ref, o_ref, m_sc, l_sc, acc_sc = rest[2 * pps:]
    j = pl.program_id(1)
    nt, att = q_ref.shape
    nhh = 2 * nheads
    dv = 2 * dk
    rows = nhh * nt

    q8 = q_ref[...]
    qt = jnp.concatenate([q8] * nhh, axis=0)
    r_hh = lax.broadcasted_iota(I32, (rows, att), 0) >> (nt.bit_length() - 1)
    c_hh = lax.broadcasted_iota(I32, (rows, att), 1) >> (dk.bit_length() - 1)
    qb = jnp.where(r_hh == c_hh, qt, 0.0).astype(BF16)
    nt_dims = (((1,), (1,)), ((), ()))

    @pl.when(j == 0)
    def _():
        m_sc[...] = jnp.full(m_sc.shape, -jnp.inf, F32)
        l_sc[...] = jnp.zeros(l_sc.shape, F32)
        acc_sc[...] = jnp.zeros(acc_sc.shape, F32)

    def update(s, vals):
        m_prev = m_sc[...]
        m_new = jnp.maximum(m_prev, jnp.max(s, axis=1, keepdims=True))
        alpha = jnp.exp2(m_prev - m_new)
        p = jnp.exp2(s - m_new).astype(BF16)
        l_sc[...] = alpha * l_sc[...] + jnp.sum(p.astype(F32), axis=1, keepdims=True)
        pv = None
        for i, v in enumerate(vals):
            d = jnp.dot(p[:, i * page:(i + 1) * page], v, preferred_element_type=F32)
            pv = d if pv is None else pv + d
        acc_sc[...] = alpha * acc_sc[...] + pv
        m_sc[...] = m_new

    s_list = [lax.dot_general(qb, k_refs[i][...].astype(BF16), nt_dims, preferred_element_type=F32)
              for i in range(pps)]
    update(jnp.concatenate(s_list, axis=1), [v_refs[i][...].astype(BF16) for i in range(pps)])

    @pl.when(j == pl.num_programs(1) - 1)
    def _():
        zpad = jnp.zeros((page - nt, att), F32)
        kn = jnp.concatenate([kn_ref[...], zpad], axis=0).astype(BF16)
        vn = jnp.concatenate([vn_ref[...], zpad], axis=0).astype(BF16)
        s = lax.dot_general(qb, kn, nt_dims, preferred_element_type=F32)
        tok = lax.broadcasted_iota(I32, s.shape, 0) & (nt - 1)
        col = lax.broadcasted_iota(I32, s.shape, 1)
        update(jnp.where(col <= tok, s, -jnp.inf), [vn])

        lam = _lambda_value(lq1, lk1, lq2, lk2, lam_init)
        o = acc_sc[...] / l_sc[...]
        for h in range(nheads):
            r0 = 2 * h * nt
            o1 = o[r0:r0 + nt, h * dv:(h + 1) * dv]
            o2 = o[r0 + nt:r0 + 2 * nt, h * dv:(h + 1) * dv]
            d = o1 - lam * o2
            dn = d * lax.rsqrt(jnp.mean(d * d, axis=-1, keepdims=True) + RMS_EPS)
            o_ref[:, h * dv:(h + 1) * dv] = dn * g_ref[...] * (1.0 - lam_init)


def _attn_sample(q, k_new, v_new, cache_k, cache_v, page_table, lams, subln_g, *,
                 nheads, dk, dec_seq, lam_init):
    t, att = q.shape
    nseq = t // dec_seq
    npool, page = cache_k.shape[0], cache_k.shape[1]
    npages = page_table.shape[1]
    pps = min(PAGES_PER_STEP, npages)
    nsteps = npages // pps
    ck = cache_k.reshape(npool, page, att)
    cv = cache_v.reshape(npool, page, att)
    tok_spec = pl.BlockSpec((dec_seq, att), lambda s, j, pt: (s, 0))
    lam_spec = pl.BlockSpec((1, dk), lambda s, j, pt: (0, 0))

    def page_spec(i):
        return pl.BlockSpec((None, page, att),
                            lambda s, j, pt: (pt[s * npages + j * pps + i], 0, 0))

    grid_spec = pltpu.PrefetchScalarGridSpec(
        num_scalar_prefetch=1,
        grid=(nseq, nsteps),
        in_specs=[tok_spec, tok_spec, tok_spec]
                 + [page_spec(i) for i in range(pps)] + [page_spec(i) for i in range(pps)]
                 + [lam_spec] * 4 + [pl.BlockSpec((1, 2 * dk), lambda s, j, pt: (0, 0))],
        out_specs=tok_spec,
        scratch_shapes=[pltpu.VMEM((2 * nheads * dec_seq, 1), F32),
                        pltpu.VMEM((2 * nheads * dec_seq, 1), F32),
                        pltpu.VMEM((2 * nheads * dec_seq, att), F32)],
    )
    return pl.pallas_call(
        functools.partial(_sattn_kernel, pps=pps, nheads=nheads, dk=dk, page=page, lam_init=lam_init),
        grid_spec=grid_spec,
        out_shape=jax.ShapeDtypeStruct((t, att), F32),
        compiler_params=_params("parallel", "arbitrary"),
        name="attn_sample",
    )(page_table.reshape(-1), q, k_new, v_new, *([ck] * pps), *([cv] * pps), *lams, subln_g)


def _post_kernel(x_ref, a_ref, yc_ref, gate_ref, shift_ref, scale_ref, woa_ref, woc_ref, g2_ref,
                 wr_ref, br_ref, tri_ref, x1_out, h2_out, ri_out, ptok_out, cnt_out, carry_sc, *, ne):
    i = pl.program_id(0)
    tm = x_ref.shape[0]

    @pl.when(i == 0)
    def _():
        carry_sc[...] = jnp.zeros(carry_sc.shape, F32)

    mix = (jnp.dot(a_ref[...].astype(BF16), woa_ref[...], preferred_element_type=F32)
           + jnp.dot(yc_ref[...], woc_ref[...], preferred_element_type=F32))
    x1 = x_ref[...] + gate_ref[...] * mix
    ms = jnp.mean(x1 * x1, axis=-1, keepdims=True)
    h2 = (x1 * lax.rsqrt(ms + RMS_EPS) * g2_ref[...]) * (1.0 + scale_ref[...]) + shift_ref[...]
    x1_out[...] = x1
    h2_out[...] = h2

    logits = _dot3(h2, wr_ref[...]) + br_ref[...]
    lt = logits.T[0:ne, :]
    eidx = lax.broadcasted_iota(I32, lt.shape, 0)
    vals, idxs = [], []
    for _ in range(TOP_K):
        mx = jnp.max(lt, axis=0, keepdims=True)
        ik = jnp.min(jnp.where(lt == mx, eidx, ne), axis=0, keepdims=True)
        vals.append(mx)
        idxs.append(ik)
        lt = jnp.where(eidx == ik, -jnp.inf, lt)
    ex = [jnp.exp(v - vals[0]) for v in vals]
    den = ex[0] + ex[1] + ex[2] + ex[3]
    probs = [e / den for e in ex]

    sel = jnp.zeros(lt.shape, F32)
    for ik in idxs:
        sel = sel + jnp.where(eidx == ik, 1.0, 0.0)
    cum = jnp.dot(sel.astype(BF16), tri_ref[...], preferred_element_type=F32) + carry_sc[...]
    ranks = [jnp.sum(jnp.where(eidx == ik, cum, 0.0), axis=0, keepdims=True) for ik in idxs]
    carry_new = carry_sc[...] + jnp.sum(sel, axis=1, keepdims=True)
    carry_sc[...] = carry_new
    cnt_out[...] = jnp.broadcast_to(carry_new, cnt_out.shape)

    ri_out[...] = jnp.concatenate(idxs + [r.astype(I32) for r in ranks], axis=0)
    ppad = jnp.concatenate(probs + [jnp.zeros((LANES - TOP_K, tm), F32)], axis=0)
    ptok_out[...] = ppad.T


def _post(x, a, yc, mod_specs, mods, woa, woc, g2, wr_pad, br_pad, *, seq_len, ne):
    t, d = x.shape
    att = a.shape[1]
    conv = yc.shape[1]
    tm = min(TOKEN_TILE, t)
    nt = t // tm
    tri = (lax.broadcasted_iota(I32, (tm, tm), 0) < lax.broadcasted_iota(I32, (tm, tm), 1)).astype(BF16)
    const = lambda shape: pl.BlockSpec(shape, lambda i: (0,) * len(shape))
    row = lambda w: pl.BlockSpec((tm, w), lambda i: (i, 0))
    return pl.pallas_call(
        functools.partial(_post_kernel, ne=ne),
        grid=(nt,),
        in_specs=[row(d), row(att), row(conv), mod_specs(2, tm), mod_specs(3, tm), mod_specs(4, tm),
                  const((att, d)), const((conv, d)), const((1, d)), const((d, LANES)), const((1, LANES)),
                  const((tm, tm))],
        out_specs=[row(d), row(d), pl.BlockSpec((2 * TOP_K, tm), lambda i: (0, i)), row(LANES),
                   const((ne, LANES))],
        out_shape=[jax.ShapeDtypeStruct((t, d), F32), jax.ShapeDtypeStruct((t, d), F32),
                   jax.ShapeDtypeStruct((2 * TOP_K, t), I32), jax.ShapeDtypeStruct((t, LANES), F32),
                   jax.ShapeDtypeStruct((ne, LANES), F32)],
        scratch_shapes=[pltpu.VMEM((ne, 1), F32)],
        compiler_params=_params("arbitrary"),
        name="post_mix",
    )(x, a, yc, mods, mods, mods, woa, woc, g2, wr_pad, br_pad, tri)


def _dispatch_kernel(dest_hbm, h_hbm, xs_in, xs_hbm, idx_smem, sem, *, tmd):
    del xs_in
    i = pl.program_id(0)
    cp = pltpu.make_async_copy(dest_hbm.at[i], idx_smem, sem.at[0])
    cp.start()
    cp.wait()
    base = i * tmd

    def issue(t, c):
        for k in range(TOP_K):
            d = idx_smem[k * tmd + t]
            pltpu.make_async_copy(h_hbm.at[pl.ds(base + t, 1)], xs_hbm.at[pl.ds(d, 1)], sem.at[1]).start()
        return c

    lax.fori_loop(0, tmd, issue, 0)

    def drain(t, c):
        for k in range(TOP_K):
            pltpu.make_async_copy(h_hbm.at[pl.ds(0, 1)], xs_hbm.at[pl.ds(0, 1)], sem.at[1]).wait()
        return c

    lax.fori_loop(0, tmd, drain, 0)


def _dispatch(dest_tiles, h2, n_rows):
    t, d = h2.shape
    nt, width = dest_tiles.shape
    tmd = width // TOP_K
    xs0 = jnp.zeros((n_rows, d), h2.dtype)
    return pl.pallas_call(
        functools.partial(_dispatch_kernel, tmd=tmd),
        grid=(nt,),
        in_specs=[pl.BlockSpec(memory_space=pl.ANY)] * 3,
        out_specs=pl.BlockSpec(memory_space=pl.ANY),
        out_shape=jax.ShapeDtypeStruct((n_rows, d), h2.dtype),
        scratch_shapes=[pltpu.SMEM((width,), I32), pltpu.SemaphoreType.DMA((2,))],
        input_output_aliases={2: 0},
        compiler_params=_params("arbitrary"),
        name="moe_dispatch",
    )(dest_tiles, h2, xs0)


def _expert_kernel(be_ref, nu_ref, x_ref, wg_ref, bg_ref, wu_ref, bu_ref, wd_ref, bd_ref, y_ref,
                   wg_sc, wu_sc, wd_sc):
    i = pl.program_id(0)
    used = i < nu_ref[0]
    prev = be_ref[jnp.maximum(i - 1, 0)]
    fresh = jnp.logical_or(i == 0, be_ref[i] != prev)

    @pl.when(jnp.logical_and(used, fresh))
    def _():
        wg_sc[...] = wg_ref[...].astype(BF16)
        wu_sc[...] = wu_ref[...].astype(BF16)
        wd_sc[...] = wd_ref[...].astype(BF16)

    @pl.when(used)
    def _():
        x = x_ref[...].astype(BF16)
        g = jnp.minimum(jnp.dot(x, wg_sc[...], preferred_element_type=F32) + bg_ref[...], SWIGLU_LIMIT)
        u = jnp.clip(jnp.dot(x, wu_sc[...], preferred_element_type=F32) + bu_ref[...],
                     -SWIGLU_LIMIT, SWIGLU_LIMIT)
        a = g * jax.nn.sigmoid(SWIGLU_ALPHA * g) * (u + 1.0)
        y_ref[...] = jnp.dot(a.astype(BF16), wd_sc[...], preferred_element_type=F32) + bd_ref[...]


def _experts(block_e, n_used, xs, w_gate, b_gate, w_up, b_up, w_down, b_down):
    n_rows, d = xs.shape
    ne, _, dff = w_gate.shape
    bm = EXPERT_BLOCK
    nblk = n_rows // bm

    def blk(i, be, nu):
        return jnp.minimum(i, nu[0] - 1)

    row_spec = pl.BlockSpec((bm, d), lambda i, be, nu: (blk(i, be, nu), 0))
    w_spec = lambda k, n: pl.BlockSpec((None, k, n), lambda i, be, nu: (be[blk(i, be, nu)], 0, 0))
    grid_spec = pltpu.PrefetchScalarGridSpec(
        num_scalar_prefetch=2,
        grid=(nblk,),
        in_specs=[row_spec, w_spec(d, dff), w_spec(1, dff), w_spec(d, dff), w_spec(1, dff),
                  w_spec(dff, d), w_spec(1, d)],
        out_specs=row_spec,
        scratch_shapes=[pltpu.VMEM((d, dff), BF16), pltpu.VMEM((d, dff), BF16), pltpu.VMEM((dff, d), BF16)],
    )
    return pl.pallas_call(
        _expert_kernel,
        grid_spec=grid_spec,
        out_shape=jax.ShapeDtypeStruct((n_rows, d), F32),
        compiler_params=_params("arbitrary"),
        name="moe_experts",
    )(block_e, n_used, xs, w_gate, b_gate.reshape(ne, 1, dff), w_up, b_up.reshape(ne, 1, dff),
      w_down, b_down.reshape(ne, 1, d))


def _combine_kernel(dest_hbm, x1_ref, gate_ref, p_ref, yb_hbm, o_ref, idx_smem, buf, sem, *, tmc):
    i = pl.program_id(0)
    cp = pltpu.make_async_copy(dest_hbm.at[i], idx_smem, sem.at[0])
    cp.start()
    cp.wait()

    def issue(t, c):
        for k in range(TOP_K):
            d = idx_smem[k * tmc + t]
            pltpu.make_async_copy(yb_hbm.at[pl.ds(d, 1)], buf.at[k, pl.ds(t, 1)], sem.at[1]).start()
        return c

    lax.fori_loop(0, tmc, issue, 0)

    def drain(t, c):
        for k in range(TOP_K):
            pltpu.make_async_copy(yb_hbm.at[pl.ds(0, 1)], buf.at[0, pl.ds(0, 1)], sem.at[1]).wait()
        return c

    lax.fori_loop(0, tmc, drain, 0)

    p = p_ref[...]
    y = p[:, 0:1] * buf[0]
    for k in range(1, TOP_K):
        y = y + p[:, k:k + 1] * buf[k]
    o_ref[...] = x1_ref[...] + gate_ref[...] * y


def _combine(dest_tiles, x1, mod_specs, mods, ptok, yb):
    t, d = x1.shape
    nt, width = dest_tiles.shape
    tmc = width // TOP_K
    row = lambda w: pl.BlockSpec((tmc, w), lambda i: (i, 0))
    return pl.pallas_call(
        functools.partial(_combine_kernel, tmc=tmc),
        grid=(nt,),
        in_specs=[pl.BlockSpec(memory_space=pl.ANY), row(d), mod_specs(5, tmc), row(LANES),
                  pl.BlockSpec(memory_space=pl.ANY)],
        out_specs=row(d),
        out_shape=jax.ShapeDtypeStruct((t, d), F32),
        scratch_shapes=[pltpu.SMEM((width,), I32), pltpu.VMEM((TOP_K, tmc, d), F32),
                        pltpu.SemaphoreType.DMA((2,))],
        compiler_params=_params("arbitrary"),
        name="moe_combine",
    )(dest_tiles, x1, mods, ptok, yb)


def _dest_tiles(dest, tile):
    t = dest.shape[1]
    return dest.reshape(TOP_K, t // tile, tile).transpose(1, 0, 2).reshape(t // tile, TOP_K * tile)


def _moe(h2, x1, ri, ptok, cnt, mod_specs, mods, w_gate, b_gate, w_up, b_up, w_down, b_down):
    t, d = h2.shape
    ne = w_gate.shape[0]
    bm = EXPERT_BLOCK
    n_rows = -(-(t * TOP_K) // bm) * bm + ne * bm
    counts = cnt[:, 0].astype(I32)
    padded = (counts + bm - 1) // bm * bm
    pend = jnp.cumsum(padded)
    pstart = pend - padded
    dest = pstart[ri[0:TOP_K]] + ri[TOP_K:2 * TOP_K]
    nblk = n_rows // bm
    block_e = jnp.minimum(jnp.searchsorted(pend, jnp.arange(nblk, dtype=I32) * bm, side="right"),
                          ne - 1).astype(I32)
    n_used = (pend[-1:] // bm).astype(I32)
    tmd = min(TOKEN_TILE, t)
    tmc = min(COMBINE_TILE, t)
    xs = _dispatch(_dest_tiles(dest, tmd), h2, n_rows)
    yb = _experts(block_e, n_used, xs, w_gate, b_gate, w_up, b_up, w_down, b_down)
    return _combine(_dest_tiles(dest, tmc), x1, mod_specs, mods, ptok, yb)


def kernel(x_prompt, x_sample, cache_k, cache_v, state_conv, page_table, c_prompt, c_sample,
           w_ada, b_ada, norm1_g, w_in, q_norm_g, k_norm_g, lambda_q1, lambda_k1, lambda_q2,
           lambda_k2, subln_g, conv_w, w_out, norm2_g, w_router, b_router, w_gate, b_gate,
           w_up, b_up, w_down, b_down):
    nb, seq, d = x_prompt.shape
    db, dec_seq, _ = x_sample.shape
    depth, _, page, nheads, dv = cache_k.shape
    dk = dv // 2
    att = nheads * dv
    conv = conv_w.shape[2]
    ne = w_router.shape[2]
    tp, ts = nb * seq, db * dec_seq
    assert dec_seq == SUBLANES and att % LANES == 0 and conv % LANES == 0 and ne <= LANES
    assert dk & (dk - 1) == 0

    c_rows = -(-(nb + db) // SUBLANES) * SUBLANES
    c_all = jnp.concatenate([c_prompt, c_sample, jnp.zeros((c_rows - nb - db, d), F32)], axis=0)
    gidx = jnp.arange(att, dtype=I32) // dk
    gmat = (gidx[:, None] == gidx[None, :]).astype(BF16)

    yp = x_prompt.reshape(tp, d)
    ys = x_sample.reshape(ts, d)
    outs = [[] for _ in range(6)]
    for l in range(depth):
        lam_init = 0.8 - 0.6 * math.exp(-0.3 * l)
        mod_all = _adaln(c_all, w_ada[l], b_ada[l])
        mod_p = mod_all[:nb].reshape(nb, N_ADA, 1, d)
        mod_s = jnp.repeat(mod_all[nb:nb + db], dec_seq, axis=0)

        def mod_specs_p(j, tile):
            per_batch = max(seq // tile, 1)
            return pl.BlockSpec((None, None, 1, d), lambda i: (i // per_batch, j, 0, 0))

        def mod_specs_s(j, tile):
            return pl.BlockSpec((tile, d), lambda i: (i, j))

        win_bf = w_in[l].astype(BF16)
        woa = w_out[l][:att].astype(BF16)
        woc = w_out[l][att:].astype(BF16)
        g1 = norm1_g[l].reshape(1, d)
        g2 = norm2_g[l].reshape(1, d)
        gq = jnp.tile(q_norm_g[l], att // dk).reshape(1, att)
        gk = jnp.tile(k_norm_g[l], att // dk).reshape(1, att)
        lams = [v[l].reshape(1, dk) for v in (lambda_q1, lambda_k1, lambda_q2, lambda_k2)]
        sg = subln_g[l].reshape(1, dv)
        wr_pad = jnp.pad(w_router[l], ((0, 0), (0, LANES - ne)))
        br_pad = jnp.pad(b_router[l], (0, LANES - ne)).reshape(1, LANES)
        moe_w = (w_gate[l], b_gate[l], w_up[l], b_up[l], w_down[l], b_down[l])

        q_bf, k_f, v_f, k_bf, v_bf, yc, conv_p = _premix(
            yp, mod_specs_p, mod_p, g1, win_bf, gq, gk, gmat, conv_w[l], None,
            sample=False, seq_len=seq, dk=dk)
        tq = min(ATTN_TILE, seq)
        nq = seq // tq
        qT = q_bf.reshape(nb, seq, att).transpose(0, 2, 1)
        vT = (v_bf.reshape(nb, nq, tq, nheads, dv).transpose(0, 3, 1, 4, 2)
              .reshape(nb, nheads * nq, dv, tq))
        a_p = _attn_prompt(qT, k_bf.reshape(nb, seq, att), vT, lams, sg,
                           nheads=nheads, dk=dk, lam_init=lam_init).reshape(tp, att)
        x1, h2, ri, ptok, cnt = _post(yp, a_p, yc, mod_specs_p, mod_p, woa, woc, g2, wr_pad, br_pad,
                                      seq_len=seq, ne=ne)
        yp = _moe(h2, x1, ri, ptok, cnt, mod_specs_p, mod_p, *moe_w)
        outs[0].append(k_f.reshape(nb, seq, nheads, dv))
        outs[1].append(v_f.reshape(nb, seq, nheads, dv))
        outs[2].append(conv_p)

        q_s, k_s, v_s, yc_s, conv_s = _premix(
            ys, mod_specs_s, mod_s, g1, win_bf, gq, gk, gmat, conv_w[l], state_conv[l],
            sample=True, seq_len=dec_seq, dk=dk)
        a_s = _attn_sample(q_s, k_s, v_s, cache_k[l], cache_v[l], page_table, lams, sg,
                           nheads=nheads, dk=dk, dec_seq=dec_seq, lam_init=lam_init)
        x1, h2, ri, ptok, cnt = _post(ys, a_s, yc_s, mod_specs_s, mod_s, woa, woc, g2, wr_pad, br_pad,
                                      seq_len=dec_seq, ne=ne)
        ys = _moe(h2, x1, ri, ptok, cnt, mod_specs_s, mod_s, *moe_w)
        outs[3].append(k_s.reshape(db, dec_seq, nheads, dv))
        outs[4].append(v_s.reshape(db, dec_seq, nheads, dv))
        outs[5].append(conv_s)

    return (yp.reshape(nb, seq, d), ys.reshape(db, dec_seq, d),
            jnp.stack(outs[0], 0), jnp.stack(outs[1], 0), jnp.stack(outs[2], 0),
            jnp.stack(outs[3], 0), jnp.stack(outs[4], 0), jnp.stack(outs[5], 0))
```

```python
import functools
import math

import jax
import jax.numpy as jnp
from jax import lax
from jax.experimental import pallas as pl
from jax.experimental.pallas import tpu as pltpu

F32 = jnp.float32
BF16 = jnp.bfloat16
I32 = jnp.int32

RMS_EPS = 1e-6
N_ADA = 6
TOP_K = 4
CONV_WIDTH = 3
SWIGLU_ALPHA = 1.702
SWIGLU_LIMIT = 7.0
LOG2E = 1.4426950408889634

LANES = 128
SUBLANES = 8
VMEM_LIMIT = 56 * 1024 * 1024

TOKEN_TILE = 512
ATTN_TILE = 2048
ATTN_KEYS = 512
ATTN_COLS = 256
EXPERT_BLOCK = 256
COMBINE_TILE = 256
PAGES_PER_STEP = 16
ROW_UNROLL = 8


def _params(*sem):
    return pltpu.CompilerParams(dimension_semantics=sem, vmem_limit_bytes=VMEM_LIMIT)


def _split_bf16(a):
    hi = a.astype(BF16)
    lo = (a - hi.astype(F32)).astype(BF16)
    return hi, lo


def _dot3(a, w):
    a_hi, a_lo = _split_bf16(a)
    w_hi, w_lo = _split_bf16(w)
    d = functools.partial(jnp.dot, preferred_element_type=F32)
    return d(a_hi, w_hi) + (d(a_hi, w_lo) + d(a_lo, w_hi))


def _adaln_kernel(c_ref, w_ref, b_ref, o_ref):
    c = c_ref[...]
    s = c * jax.nn.sigmoid(c)
    o_ref[...] = _dot3(s, w_ref[...]) + b_ref[...]


def _adaln(c_all, w_ada, b_ada):
    rows, d = c_all.shape
    n = w_ada.shape[1]
    tn = d
    return pl.pallas_call(
        _adaln_kernel,
        grid=(n // tn,),
        in_specs=[pl.BlockSpec((rows, d), lambda j: (0, 0)),
                  pl.BlockSpec((d, tn), lambda j: (0, j)),
                  pl.BlockSpec((1, tn), lambda j: (0, j))],
        out_specs=pl.BlockSpec((rows, tn), lambda j: (0, j)),
        out_shape=jax.ShapeDtypeStruct((rows, n), F32),
        compiler_params=_params("parallel"),
        name="adaln",
    )(c_all, w_ada, b_ada.reshape(1, n))


def _premix_kernel(x_ref, shift_ref, scale_ref, g1_ref, win_ref, gq_ref, gk_ref, gmat_ref, cw_ref,
                   *rest, sample, tiles_per_batch, att, dk, q_scale):
    if sample:
        st_ref, q_out, kf_out, vf_out, yc_out, cs_out, ubuf = rest
    else:
        q_out, kf_out, vf_out, kb_out, vb_out, yc_out, cs_out, ubuf = rest
    tm = x_ref.shape[0]
    conv = yc_out.shape[1]

    x = x_ref[...]
    ms = jnp.mean(x * x, axis=-1, keepdims=True)
    xn = x * lax.rsqrt(ms + RMS_EPS) * g1_ref[...]
    h = (xn * (1.0 + scale_ref[...]) + shift_ref[...]).astype(BF16)
    z = jnp.dot(h, win_ref[...], preferred_element_type=F32)

    def group_norm(t, g_ref):
        ss = jnp.dot((t * t).astype(BF16), gmat_ref[...], preferred_element_type=F32)
        return t * lax.rsqrt(ss * (1.0 / dk) + RMS_EPS) * g_ref[...]

    q = group_norm(z[:, 0:att], gq_ref)
    k = group_norm(z[:, att:2 * att], gk_ref)
    v = z[:, 2 * att:3 * att]
    cx = z[:, 3 * att:3 * att + conv]
    cb = z[:, 3 * att + conv:3 * att + 2 * conv]
    cc = z[:, 3 * att + 2 * conv:3 * att + 3 * conv]

    q_out[...] = (q * q_scale).astype(q_out.dtype)
    kf_out[...] = k
    vf_out[...] = v
    if not sample:
        kb_out[...] = k.astype(BF16)
        vb_out[...] = v.astype(BF16)

    u = cc * cx
    w0 = cw_ref[0:1, :]
    w1 = cw_ref[1:2, :]
    w2 = cw_ref[2:3, :]
    if sample:
        nseq = tm // SUBLANES
        u3 = u.reshape(nseq, SUBLANES, conv)
        ubuf[:, 8:16, :] = u3
        ubuf[:, 6:8, :] = st_ref[...]
        um1 = ubuf[:, 7:15, :]
        um2 = ubuf[:, 6:14, :]
        y3 = w0 * um2 + w1 * um1 + w2 * u3
        y = y3.reshape(tm, conv)
        cs_out[...] = ubuf[:, 14:16, :]
    else:
        i = pl.program_id(0)

        @pl.when(i % tiles_per_batch == 0)
        def _():
            ubuf[0:8, :] = jnp.zeros((8, conv), F32)

        ubuf[8:tm + 8, :] = u
        um1 = ubuf[7:tm + 7, :]
        um2 = ubuf[6:tm + 6, :]
        y = w0 * um2 + w1 * um1 + w2 * u
        ubuf[0:8, :] = ubuf[tm:tm + 8, :]
        cs_out[...] = u[tm - 2:tm, :]
    yc_out[...] = (cb * y).astype(BF16)


def _premix(x, mod_specs, mods, g1, win_bf, gq, gk, gmat, conv_w, state, *, sample, seq_len, dk):
    t, d = x.shape
    att = gq.shape[1]
    conv = conv_w.shape[1]
    ncol = win_bf.shape[1]
    tm = min(TOKEN_TILE, t)
    nt = t // tm
    tiles_per_batch = max(seq_len // tm, 1)
    q_scale = dk ** -0.5 * LOG2E
    const = lambda shape: pl.BlockSpec(shape, lambda i: (0,) * len(shape))
    row = lambda w: pl.BlockSpec((tm, w), lambda i: (i, 0))
    in_specs = [row(d), mod_specs(0, tm), mod_specs(1, tm), const((1, d)), const((d, ncol)),
                const((1, att)), const((1, att)), const((att, att)), const((CONV_WIDTH, conv))]
    args = [x, mods, mods, g1, win_bf, gq, gk, gmat, conv_w]
    if sample:
        nseq = tm // SUBLANES
        in_specs.append(pl.BlockSpec((nseq, CONV_WIDTH - 1, conv), lambda i: (i, 0, 0)))
        args.append(state)
        out_specs = [row(att), row(att), row(att), row(conv),
                     pl.BlockSpec((nseq, CONV_WIDTH - 1, conv), lambda i: (i, 0, 0))]
        out_shape = [jax.ShapeDtypeStruct((t, att), F32), jax.ShapeDtypeStruct((t, att), F32),
                     jax.ShapeDtypeStruct((t, att), F32), jax.ShapeDtypeStruct((t, conv), BF16),
                     jax.ShapeDtypeStruct((t // SUBLANES, CONV_WIDTH - 1, conv), F32)]
        scratch = [pltpu.VMEM((nseq, 2 * SUBLANES, conv), F32)]
    else:
        nb = t // seq_len
        out_specs = [row(att), row(att), row(att), row(att), row(att), row(conv),
                     pl.BlockSpec((None, CONV_WIDTH - 1, conv), lambda i: (i // tiles_per_batch, 0, 0))]
        out_shape = [jax.ShapeDtypeStruct((t, att), BF16), jax.ShapeDtypeStruct((t, att), F32),
                     jax.ShapeDtypeStruct((t, att), F32), jax.ShapeDtypeStruct((t, att), BF16),
                     jax.ShapeDtypeStruct((t, att), BF16), jax.ShapeDtypeStruct((t, conv), BF16),
                     jax.ShapeDtypeStruct((nb, CONV_WIDTH - 1, conv), F32)]
        scratch = [pltpu.VMEM((tm + SUBLANES, conv), F32)]
    return pl.pallas_call(
        functools.partial(_premix_kernel, sample=sample, tiles_per_batch=tiles_per_batch,
                          att=att, dk=dk, q_scale=q_scale),
        grid=(nt,),
        in_specs=in_specs,
        out_specs=out_specs,
        out_shape=out_shape,
        scratch_shapes=scratch,
        compiler_params=_params("arbitrary"),
        name="premix_sample" if sample else "premix_prompt",
    )(*args)


def _lambda_value(lq1, lk1, lq2, lk2, lam_init):
    a = jnp.sum(lq1[...] * lk1[...], axis=-1, keepdims=True)
    b = jnp.sum(lq2[...] * lk2[...], axis=-1, keepdims=True)
    return jnp.exp(a) - jnp.exp(b) + lam_init


def _attn_kernel(qT_ref, k_ref, vT_ref, lq1, lk1, lq2, lk2, g_ref, o_ref,
                 acc1, acc2, m1, l1, m2, l2, *, dk, lam_init):
    tq = qT_ref.shape[1]
    tk = vT_ref.shape[2]
    qi = pl.program_id(2)
    qT = qT_ref[...].astype(F32)
    half = lax.broadcasted_iota(I32, qT.shape, 0) < dk
    rhs = (jnp.where(half, qT, 0.0).astype(BF16), jnp.where(half, 0.0, qT).astype(BF16))
    accs, ms, ls = (acc1, acc2), (m1, m2), (l1, l2)
    for hf in range(2):
        accs[hf][...] = jnp.zeros(accs[hf].shape, F32)
        ms[hf][...] = jnp.full(ms[hf].shape, -jnp.inf, F32)
        ls[hf][...] = jnp.zeros(ls[hf].shape, F32)

    cw = min(ATTN_COLS, tq)
    units = [(hf, cb) for hf in range(2) for cb in range(tq // cw)]

    def qk(kc, u):
        hf, cb = u
        return jnp.dot(kc, rhs[hf][:, cb * cw:(cb + 1) * cw], preferred_element_type=F32)

    def softmax(s, u, diag):
        hf, cb = u
        cols = slice(cb * cw, (cb + 1) * cw)
        if diag is not None and cb * cw < (diag + 1) * tk - 1:
            kpos = lax.broadcasted_iota(I32, s.shape, 0) + diag * tk
            qpos = lax.broadcasted_iota(I32, s.shape, 1) + cb * cw
            s = jnp.where(kpos <= qpos, s, -jnp.inf)
        m_prev = ms[hf][:, cols]
        m_new = jnp.maximum(m_prev, jnp.max(s, axis=0, keepdims=True))
        alpha = jnp.exp2(m_prev - m_new)
        p = jnp.exp2(s - m_new)
        ls[hf][:, cols] = alpha * ls[hf][:, cols] + jnp.sum(p, axis=0, keepdims=True)
        ms[hf][:, cols] = m_new
        return p.astype(BF16), alpha

    def pv(vc, pa, u):
        hf, cb = u
        cols = slice(cb * cw, (cb + 1) * cw)
        p, alpha = pa
        accs[hf][:, cols] = alpha * accs[hf][:, cols] + jnp.dot(vc, p, preferred_element_type=F32)

    def step(j, diag):
        kc = k_ref[pl.ds(pl.multiple_of(j * tk, tk), tk), :]
        vc = vT_ref[j]
        live = [u for u in units if diag is None or (u[1] + 1) * cw > diag * tk]
        n = len(live)
        s, pa = {}, {}
        for t in range(n + 2):
            if t < n:
                s[t] = qk(kc, live[t])
            if 0 <= t - 1 < n:
                pa[t - 1] = softmax(s.pop(t - 1), live[t - 1], diag)
            if 0 <= t - 2 < n:
                pv(vc, pa.pop(t - 2), live[t - 2])

    def body(j, c):
        step(j, None)
        return c

    kpt = tq // tk
    lax.fori_loop(0, qi * kpt, body, 0)
    for c in range(kpt):
        step(qi * kpt + c, c)

    lam = _lambda_value(lq1, lk1, lq2, lk2, lam_init)
    o = acc1[...] / l1[...] - lam * (acc2[...] / l2[...])
    on = o * lax.rsqrt(jnp.mean(o * o, axis=0, keepdims=True) + RMS_EPS) * (1.0 - lam_init)
    o_ref[...] = (on.T * g_ref[...]).astype(o_ref.dtype)


def _attn_prompt(qT, k_bf, vT, lams, subln_g, *, nheads, dk, lam_init):
    b, att, s = qT.shape
    dv = att // nheads
    tq = min(ATTN_TILE, s)
    nq = s // tq
    nk, tk = vT.shape[1] // nheads, vT.shape[3]
    lam_spec = pl.BlockSpec((1, dk), lambda bi, hi, qi: (0, 0))
    return pl.pallas_call(
        functools.partial(_attn_kernel, dk=dk, lam_init=lam_init),
        grid=(b, nheads, nq),
        in_specs=[pl.BlockSpec((None, dv, tq), lambda bi, hi, qi: (bi, hi, qi)),
                  pl.BlockSpec((None, s, dv), lambda bi, hi, qi: (bi, 0, hi)),
                  pl.BlockSpec((None, nk, dv, tk), lambda bi, hi, qi: (bi, hi, 0, 0)),
                  lam_spec, lam_spec, lam_spec, lam_spec,
                  pl.BlockSpec((1, dv), lambda bi, hi, qi: (0, 0))],
        out_specs=pl.BlockSpec((None, tq, dv), lambda bi, hi, qi: (bi, qi, hi)),
        out_shape=jax.ShapeDtypeStruct((b, s, att), BF16),
        scratch_shapes=[pltpu.VMEM((dv, tq), F32), pltpu.VMEM((dv, tq), F32),
                        pltpu.VMEM((1, tq), F32), pltpu.VMEM((1, tq), F32),
                        pltpu.VMEM((1, tq), F32), pltpu.VMEM((1, tq), F32)],
        compiler_params=_params("parallel", "parallel", "arbitrary"),
        name="attn_prompt",
    )(qT, k_bf, vT, *lams, subln_g)


def _sattn_kernel(pt_ref, q_ref, kn_ref, vn_ref, *rest, pps, nheads, dk, lam_init):
    k_refs = rest[:pps]
    v_refs = rest[pps:2 * pps]
    lq1, lk1, lq2, lk2, g_ref, o_ref, m_sc, l_sc, acc_sc = rest[2 * pps:]
    j = pl.program_id(1)
    nt = q_ref.shape[0]
    dv = 2 * dk
    prow = k_refs[0].shape[0]
    rows = 2 * nheads * nt
    nt_dims = (((1,), (1,)), ((), ()))
    head_shift = (2 * nt).bit_length() - 1

    q8 = q_ref[...]
    first = lax.broadcasted_iota(I32, (nt, dv), 1) < dk
    pieces = []
    for h in range(nheads):
        qh = q8[:, h * dv:(h + 1) * dv]
        pieces += [jnp.where(first, qh, 0.0), jnp.where(first, 0.0, qh)]
    qall = jnp.concatenate(pieces, axis=0).astype(BF16)

    @pl.when(j == 0)
    def _():
        m_sc[...] = jnp.full(m_sc.shape, -jnp.inf, F32)
        l_sc[...] = jnp.zeros(l_sc.shape, F32)
        acc_sc[...] = jnp.zeros(acc_sc.shape, F32)

    def update(s, vals, width):
        m_prev = m_sc[...]
        m_new = jnp.maximum(m_prev, jnp.max(s, axis=1, keepdims=True))
        alpha = jnp.exp2(m_prev - m_new)
        p = jnp.exp2(s - m_new).astype(BF16)
        l_sc[...] = alpha * l_sc[...] + jnp.sum(p.astype(F32), axis=1, keepdims=True)
        pv = None
        for i, v in enumerate(vals):
            d = jnp.dot(p[:, i * width:(i + 1) * width], v, preferred_element_type=F32)
            pv = d if pv is None else pv + d
        acc_sc[...] = alpha * acc_sc[...] + pv
        m_sc[...] = m_new

    row_head = lax.broadcasted_iota(I32, (rows, prow), 0) >> head_shift
    col_head = lax.broadcasted_iota(I32, (rows, prow), 1) & (nheads - 1)
    own = row_head == col_head
    s_list = [jnp.where(own, lax.dot_general(qall, k_refs[i][...].astype(BF16), nt_dims,
                                             preferred_element_type=F32), -jnp.inf)
              for i in range(pps)]
    update(jnp.concatenate(s_list, axis=1),
           [v_refs[i][...].astype(BF16) for i in range(pps)], prow)

    @pl.when(j == pl.num_programs(1) - 1)
    def _():
        npad = LANES - nheads * nt
        zpad = jnp.zeros((npad, dv), F32)
        kn = jnp.concatenate([kn_ref[:, h * dv:(h + 1) * dv] for h in range(nheads)] + [zpad],
                             axis=0).astype(BF16)
        vn = jnp.concatenate([vn_ref[:, h * dv:(h + 1) * dv] for h in range(nheads)] + [zpad],
                             axis=0).astype(BF16)
        sn = lax.dot_general(qall, kn, nt_dims, preferred_element_type=F32)
        r = lax.broadcasted_iota(I32, sn.shape, 0)
        c = lax.broadcasted_iota(I32, sn.shape, 1)
        vis = jnp.logical_and((r >> head_shift) == (c >> (nt.bit_length() - 1)),
                              (c & (nt - 1)) <= (r & (nt - 1)))
        update(jnp.where(vis, sn, -jnp.inf), [vn], LANES)

        lam = _lambda_value(lq1, lk1, lq2, lk2, lam_init)
        o = acc_sc[...] / l_sc[...]
        for h in range(nheads):
            r0 = 2 * h * nt
            d = o[r0:r0 + nt, :] - lam * o[r0 + nt:r0 + 2 * nt, :]
            dn = d * lax.rsqrt(jnp.mean(d * d, axis=-1, keepdims=True) + RMS_EPS)
            o_ref[:, h * dv:(h + 1) * dv] = dn * g_ref[...] * (1.0 - lam_init)


def _attn_sample(q, k_new, v_new, cache_k, cache_v, page_table, lams, subln_g, *,
                 nheads, dk, dec_seq, lam_init):
    t, att = q.shape
    nseq = t // dec_seq
    dv = 2 * dk
    npool, page = cache_k.shape[0], cache_k.shape[1]
    npages = page_table.shape[1]
    pps = min(PAGES_PER_STEP, npages)
    nsteps = npages // pps
    prow = page * nheads
    assert nheads & (nheads - 1) == 0 and nheads * dec_seq <= LANES
    ck = cache_k.reshape(npool, prow, dv)
    cv = cache_v.reshape(npool, prow, dv)
    tok_spec = pl.BlockSpec((dec_seq, att), lambda s, j, pt: (s, 0))
    lam_spec = pl.BlockSpec((1, dk), lambda s, j, pt: (0, 0))

    def page_spec(i):
        return pl.BlockSpec((None, prow, dv),
                            lambda s, j, pt: (pt[s * npages + j * pps + i], 0, 0))

    rows = 2 * nheads * dec_seq
    grid_spec = pltpu.PrefetchScalarGridSpec(
        num_scalar_prefetch=1,
        grid=(nseq, nsteps),
        in_specs=[tok_spec, tok_spec, tok_spec]
                 + [page_spec(i) for i in range(pps)] + [page_spec(i) for i in range(pps)]
                 + [lam_spec] * 4 + [pl.BlockSpec((1, dv), lambda s, j, pt: (0, 0))],
        out_specs=tok_spec,
        scratch_shapes=[pltpu.VMEM((rows, 1), F32), pltpu.VMEM((rows, 1), F32),
                        pltpu.VMEM((rows, dv), F32)],
    )
    return pl.pallas_call(
        functools.partial(_sattn_kernel, pps=pps, nheads=nheads, dk=dk, lam_init=lam_init),
        grid_spec=grid_spec,
        out_shape=jax.ShapeDtypeStruct((t, att), F32),
        compiler_params=_params("parallel", "arbitrary"),
        name="attn_sample",
    )(page_table.reshape(-1), q, k_new, v_new, *([ck] * pps), *([cv] * pps), *lams, subln_g)


def _post_kernel(x_ref, a_ref, yc_ref, gate_ref, shift_ref, scale_ref, woa_ref, woc_ref, g2_ref,
                 wr_ref, br_ref, tri_ref, x1_out, h2_out, ri_out, ptok_out, cnt_out, carry_sc, *, ne):
    i = pl.program_id(0)
    tm = x_ref.shape[0]

    @pl.when(i == 0)
    def _():
        carry_sc[...] = jnp.zeros(carry_sc.shape, F32)

    mix = (jnp.dot(a_ref[...].astype(BF16), woa_ref[...], preferred_element_type=F32)
           + jnp.dot(yc_ref[...], woc_ref[...], preferred_element_type=F32))
    x1 = x_ref[...] + gate_ref[...] * mix
    ms = jnp.mean(x1 * x1, axis=-1, keepdims=True)
    h2 = (x1 * lax.rsqrt(ms + RMS_EPS) * g2_ref[...]) * (1.0 + scale_ref[...]) + shift_ref[...]
    x1_out[...] = x1
    h2_out[...] = h2

    logits = _dot3(h2, wr_ref[...]) + br_ref[...]
    lt = logits.T[0:ne, :]
    eidx = lax.broadcasted_iota(I32, lt.shape, 0)
    vals, idxs = [], []
    for _ in range(TOP_K):
        mx = jnp.max(lt, axis=0, keepdims=True)
        ik = jnp.min(jnp.where(lt == mx, eidx, ne), axis=0, keepdims=True)
        vals.append(mx)
        idxs.append(ik)
        lt = jnp.where(eidx == ik, -jnp.inf, lt)
    ex = [jnp.exp(v - vals[0]) for v in vals]
    den = ex[0] + ex[1] + ex[2] + ex[3]
    probs = [e / den for e in ex]

    sel = jnp.zeros(lt.shape, F32)
    for ik in idxs:
        sel = sel + jnp.where(eidx == ik, 1.0, 0.0)
    cum = jnp.dot(sel.astype(BF16), tri_ref[...], preferred_element_type=F32) + carry_sc[...]
    ranks = [jnp.sum(jnp.where(eidx == ik, cum, 0.0), axis=0, keepdims=True) for ik in idxs]
    carry_new = carry_sc[...] + jnp.sum(sel, axis=1, keepdims=True)
    carry_sc[...] = carry_new
    cnt_out[...] = jnp.broadcast_to(carry_new, cnt_out.shape)

    ri_out[...] = jnp.concatenate(idxs + [r.astype(I32) for r in ranks], axis=0)
    ppad = jnp.concatenate(probs + [jnp.zeros((LANES - TOP_K, tm), F32)], axis=0)
    ptok_out[...] = ppad.T


def _post(x, a, yc, mod_specs, mods, woa, woc, g2, wr_pad, br_pad, *, seq_len, ne):
    t, d = x.shape
    att = a.shape[1]
    conv = yc.shape[1]
    tm = min(TOKEN_TILE, t)
    nt = t // tm
    tri = (lax.broadcasted_iota(I32, (tm, tm), 0) < lax.broadcasted_iota(I32, (tm, tm), 1)).astype(BF16)
    const = lambda shape: pl.BlockSpec(shape, lambda i: (0,) * len(shape))
    row = lambda w: pl.BlockSpec((tm, w), lambda i: (i, 0))
    return pl.pallas_call(
        functools.partial(_post_kernel, ne=ne),
        grid=(nt,),
        in_specs=[row(d), row(att), row(conv), mod_specs(2, tm), mod_specs(3, tm), mod_specs(4, tm),
                  const((att, d)), const((conv, d)), const((1, d)), const((d, LANES)), const((1, LANES)),
                  const((tm, tm))],
        out_specs=[row(d), row(d), pl.BlockSpec((2 * TOP_K, tm), lambda i: (0, i)), row(LANES),
                   const((ne, LANES))],
        out_shape=[jax.ShapeDtypeStruct((t, d), F32), jax.ShapeDtypeStruct((t, d), F32),
                   jax.ShapeDtypeStruct((2 * TOP_K, t), I32), jax.ShapeDtypeStruct((t, LANES), F32),
                   jax.ShapeDtypeStruct((ne, LANES), F32)],
        scratch_shapes=[pltpu.VMEM((ne, 1), F32)],
        compiler_params=_params("arbitrary"),
        name="post_mix",
    )(x, a, yc, mods, mods, mods, woa, woc, g2, wr_pad, br_pad, tri)


def _dispatch_kernel(dest_hbm, h_ref, xs_in, xs_hbm, idx_smem, sem, *, tmd):
    del xs_in
    i = pl.program_id(0)
    cp = pltpu.make_async_copy(dest_hbm.at[i], idx_smem, sem.at[0])
    cp.start()
    cp.wait()

    def issue(g, c):
        for u in range(ROW_UNROLL):
            t = g * ROW_UNROLL + u
            for k in range(TOP_K):
                d = idx_smem[k * tmd + t]
                pltpu.make_async_copy(h_ref.at[pl.ds(t, 1)], xs_hbm.at[pl.ds(d, 1)], sem.at[1]).start()
        return c

    lax.fori_loop(0, tmd // ROW_UNROLL, issue, 0)
    for k in range(TOP_K):
        pltpu.make_async_copy(h_ref, xs_hbm.at[pl.ds(0, tmd)], sem.at[1]).wait()


def _dispatch(dest_tiles, h2, n_rows):
    t, d = h2.shape
    nt, width = dest_tiles.shape
    tmd = width // TOP_K
    xs0 = jnp.zeros((n_rows, d), h2.dtype)
    return pl.pallas_call(
        functools.partial(_dispatch_kernel, tmd=tmd),
        grid=(nt,),
        in_specs=[pl.BlockSpec(memory_space=pl.ANY), pl.BlockSpec((tmd, d), lambda i: (i, 0)),
                  pl.BlockSpec(memory_space=pl.ANY)],
        out_specs=pl.BlockSpec(memory_space=pl.ANY),
        out_shape=jax.ShapeDtypeStruct((n_rows, d), h2.dtype),
        scratch_shapes=[pltpu.SMEM((width,), I32), pltpu.SemaphoreType.DMA((2,))],
        input_output_aliases={2: 0},
        compiler_params=_params("arbitrary"),
        name="moe_dispatch",
    )(dest_tiles, h2, xs0)


def _expert_kernel(be_ref, nu_ref, x_ref, wg_ref, bg_ref, wu_ref, bu_ref, wd_ref, bd_ref, y_ref,
                   wg_sc, wu_sc, wd_sc):
    i = pl.program_id(0)
    used = i < nu_ref[0]
    prev = be_ref[jnp.maximum(i - 1, 0)]
    fresh = jnp.logical_or(i == 0, be_ref[i] != prev)

    @pl.when(jnp.logical_and(used, fresh))
    def _():
        wg_sc[...] = wg_ref[...].astype(BF16)
        wu_sc[...] = wu_ref[...].astype(BF16)
        wd_sc[...] = wd_ref[...].astype(BF16)

    @pl.when(used)
    def _():
        x = x_ref[...].astype(BF16)
        g = jnp.minimum(jnp.dot(x, wg_sc[...], preferred_element_type=F32) + bg_ref[...], SWIGLU_LIMIT)
        u = jnp.clip(jnp.dot(x, wu_sc[...], preferred_element_type=F32) + bu_ref[...],
                     -SWIGLU_LIMIT, SWIGLU_LIMIT)
        a = g * jax.nn.sigmoid(SWIGLU_ALPHA * g) * (u + 1.0)
        y_ref[...] = jnp.dot(a.astype(BF16), wd_sc[...], preferred_element_type=F32) + bd_ref[...]

    @pl.when(jnp.logical_not(used))
    def _():
        y_ref[...] = jnp.zeros(y_ref.shape, F32)


def _experts(block_e, n_used, xs, w_gate, b_gate, w_up, b_up, w_down, b_down):
    n_rows, d = xs.shape
    ne, _, dff = w_gate.shape
    bm = EXPERT_BLOCK
    nblk = n_rows // bm

    def blk(i, be, nu):
        return jnp.minimum(i, nu[0] - 1)

    row_spec = pl.BlockSpec((bm, d), lambda i, be, nu: (blk(i, be, nu), 0))
    w_spec = lambda k, n: pl.BlockSpec((None, k, n), lambda i, be, nu: (be[blk(i, be, nu)], 0, 0))
    grid_spec = pltpu.PrefetchScalarGridSpec(
        num_scalar_prefetch=2,
        grid=(nblk,),
        in_specs=[row_spec, w_spec(d, dff), w_spec(1, dff), w_spec(d, dff), w_spec(1, dff),
                  w_spec(dff, d), w_spec(1, d)],
        out_specs=pl.BlockSpec((bm, d), lambda i, be, nu: (i, 0)),
        scratch_shapes=[pltpu.VMEM((d, dff), BF16), pltpu.VMEM((d, dff), BF16), pltpu.VMEM((dff, d), BF16)],
    )
    return pl.pallas_call(
        _expert_kernel,
        grid_spec=grid_spec,
        out_shape=jax.ShapeDtypeStruct((n_rows, d), F32),
        compiler_params=_params("arbitrary"),
        name="moe_experts",
    )(block_e, n_used, xs, w_gate, b_gate.reshape(ne, 1, dff), w_up, b_up.reshape(ne, 1, dff),
      w_down, b_down.reshape(ne, 1, d))


def _combine_kernel(dest_hbm, x1_ref, gate_ref, p_ref, yb_hbm, o_ref, idx_smem, buf, sem, *, tmc):
    i = pl.program_id(0)
    cp = pltpu.make_async_copy(dest_hbm.at[i], idx_smem, sem.at[0])
    cp.start()
    cp.wait()

    def issue(g, c):
        for u in range(ROW_UNROLL):
            t = g * ROW_UNROLL + u
            for k in range(TOP_K):
                d = idx_smem[k * tmc + t]
                pltpu.make_async_copy(yb_hbm.at[pl.ds(d, 1)], buf.at[k, pl.ds(t, 1)], sem.at[1]).start()
        return c

    lax.fori_loop(0, tmc // ROW_UNROLL, issue, 0)
    for k in range(TOP_K):
        pltpu.make_async_copy(yb_hbm.at[pl.ds(0, tmc)], buf.at[k], sem.at[1]).wait()

    p = p_ref[...]
    y = p[:, 0:1] * buf[0]
    for k in range(1, TOP_K):
        y = y + p[:, k:k + 1] * buf[k]
    o_ref[...] = x1_ref[...] + gate_ref[...] * y


def _combine(dest_tiles, x1, mod_specs, mods, ptok, yb):
    t, d = x1.shape
    nt, width = dest_tiles.shape
    tmc = width // TOP_K
    row = lambda w: pl.BlockSpec((tmc, w), lambda i: (i, 0))
    return pl.pallas_call(
        functools.partial(_combine_kernel, tmc=tmc),
        grid=(nt,),
        in_specs=[pl.BlockSpec(memory_space=pl.ANY), row(d), mod_specs(5, tmc), row(LANES),
                  pl.BlockSpec(memory_space=pl.ANY)],
        out_specs=row(d),
        out_shape=jax.ShapeDtypeStruct((t, d), F32),
        scratch_shapes=[pltpu.SMEM((width,), I32), pltpu.VMEM((TOP_K, tmc, d), F32),
                        pltpu.SemaphoreType.DMA((2,))],
        compiler_params=_params("arbitrary"),
        name="moe_combine",
    )(dest_tiles, x1, mods, ptok, yb)


def _dest_tiles(dest, tile):
    t = dest.shape[1]
    return dest.reshape(TOP_K, t // tile, tile).transpose(1, 0, 2).reshape(t // tile, TOP_K * tile)


def _moe(h2, x1, ri, ptok, cnt, mod_specs, mods, w_gate, b_gate, w_up, b_up, w_down, b_down):
    t, d = h2.shape
    ne = w_gate.shape[0]
    bm = EXPERT_BLOCK
    n_rows = -(-(t * TOP_K) // bm) * bm + ne * bm
    counts = cnt[:, 0].astype(I32)
    padded = (counts + bm - 1) // bm * bm
    pend = jnp.cumsum(padded)
    pstart = pend - padded
    eids = jnp.arange(ne, dtype=I32)
    dest = ri[TOP_K:2 * TOP_K] + jnp.sum(
        jnp.where(ri[0:TOP_K, :, None] == eids, pstart, 0), axis=-1)
    nblk = n_rows // bm
    first_row = jnp.arange(nblk, dtype=I32) * bm
    block_e = jnp.minimum(jnp.sum((pend[None, :] <= first_row[:, None]).astype(I32), axis=1), ne - 1)
    n_used = (pend[-1:] // bm).astype(I32)
    tmd = min(TOKEN_TILE, t)
    tmc = min(COMBINE_TILE, t)
    xs = _dispatch(_dest_tiles(dest, tmd), h2, n_rows)
    yb = _experts(block_e, n_used, xs, w_gate, b_gate, w_up, b_up, w_down, b_down)
    return _combine(_dest_tiles(dest, tmc), x1, mod_specs, mods, ptok, yb)


def kernel(x_prompt, x_sample, cache_k, cache_v, state_conv, page_table, c_prompt, c_sample,
           w_ada, b_ada, norm1_g, w_in, q_norm_g, k_norm_g, lambda_q1, lambda_k1, lambda_q2,
           lambda_k2, subln_g, conv_w, w_out, norm2_g, w_router, b_router, w_gate, b_gate,
           w_up, b_up, w_down, b_down):
    nb, seq, d = x_prompt.shape
    db, dec_seq, _ = x_sample.shape
    depth, _, page, nheads, dv = cache_k.shape
    dk = dv // 2
    att = nheads * dv
    conv = conv_w.shape[2]
    ne = w_router.shape[2]
    tp, ts = nb * seq, db * dec_seq
    assert dec_seq == SUBLANES and att % LANES == 0 and conv % LANES == 0 and ne <= LANES
    assert dk & (dk - 1) == 0

    c_rows = -(-(nb + db) // SUBLANES) * SUBLANES
    c_all = jnp.concatenate([c_prompt, c_sample, jnp.zeros((c_rows - nb - db, d), F32)], axis=0)
    gidx = jnp.arange(att, dtype=I32) // dk
    gmat = (gidx[:, None] == gidx[None, :]).astype(BF16)

    yp = x_prompt.reshape(tp, d)
    ys = x_sample.reshape(ts, d)
    outs = [[] for _ in range(6)]
    for l in range(depth):
        lam_init = 0.8 - 0.6 * math.exp(-0.3 * l)
        mod_all = _adaln(c_all, w_ada[l], b_ada[l])
        mod_p = mod_all[:nb].reshape(nb, N_ADA, 1, d)
        mod_s = jnp.repeat(mod_all[nb:nb + db], dec_seq, axis=0)

        def mod_specs_p(j, tile):
            per_batch = max(seq // tile, 1)
            return pl.BlockSpec((None, None, 1, d), lambda i: (i // per_batch, j, 0, 0))

        def mod_specs_s(j, tile):
            return pl.BlockSpec((tile, d), lambda i: (i, j))

        win_bf = w_in[l].astype(BF16)
        woa = w_out[l][:att].astype(BF16)
        woc = w_out[l][att:].astype(BF16)
        g1 = norm1_g[l].reshape(1, d)
        g2 = norm2_g[l].reshape(1, d)
        gq = jnp.tile(q_norm_g[l], att // dk).reshape(1, att)
        gk = jnp.tile(k_norm_g[l], att // dk).reshape(1, att)
        lams = [v[l].reshape(1, dk) for v in (lambda_q1, lambda_k1, lambda_q2, lambda_k2)]
        sg = subln_g[l].reshape(1, dv)
        wr_pad = jnp.pad(w_router[l], ((0, 0), (0, LANES - ne)))
        br_pad = jnp.pad(b_router[l], (0, LANES - ne)).reshape(1, LANES)
        moe_w = (w_gate[l], b_gate[l], w_up[l], b_up[l], w_down[l], b_down[l])

        q_bf, k_f, v_f, k_bf, v_bf, yc, conv_p = _premix(
            yp, mod_specs_p, mod_p, g1, win_bf, gq, gk, gmat, conv_w[l], None,
            sample=False, seq_len=seq, dk=dk)
        tk = min(ATTN_KEYS, seq)
        nk = seq // tk
        qT = q_bf.reshape(nb, seq, att).transpose(0, 2, 1)
        vT = (v_bf.reshape(nb, nk, tk, nheads, dv).transpose(0, 3, 1, 4, 2)
              .reshape(nb, nheads * nk, dv, tk))
        a_p = _attn_prompt(qT, k_bf.reshape(nb, seq, att), vT, lams, sg,
                           nheads=nheads, dk=dk, lam_init=lam_init).reshape(tp, att)
        x1, h2, ri, ptok, cnt = _post(yp, a_p, yc, mod_specs_p, mod_p, woa, woc, g2, wr_pad, br_pad,
                                      seq_len=seq, ne=ne)
        yp = _moe(h2, x1, ri, ptok, cnt, mod_specs_p, mod_p, *moe_w)
        outs[0].append(k_f.reshape(nb, seq, nheads, dv))
        outs[1].append(v_f.reshape(nb, seq, nheads, dv))
        outs[2].append(conv_p)

        q_s, k_s, v_s, yc_s, conv_s = _premix(
            ys, mod_specs_s, mod_s, g1, win_bf, gq, gk, gmat, conv_w[l], state_conv[l],
            sample=True, seq_len=dec_seq, dk=dk)
        a_s = _attn_sample(q_s, k_s, v_s, cache_k[l], cache_v[l], page_table, lams, sg,
                           nheads=nheads, dk=dk, dec_seq=dec_seq, lam_init=lam_init)
        x1, h2, ri, ptok, cnt = _post(ys, a_s, yc_s, mod_specs_s, mod_s, woa, woc, g2, wr_pad, br_pad,
                                      seq_len=dec_seq, ne=ne)
        ys = _moe(h2, x1, ri, ptok, cnt, mod_specs_s, mod_s, *moe_w)
        outs[3].append(k_s.reshape(db, dec_seq, nheads, dv))
        outs[4].append(v_s.reshape(db, dec_seq, nheads, dv))
        outs[5].append(conv_s)

    return (yp.reshape(nb, seq, d), ys.reshape(db, dec_seq, d),
            jnp.stack(outs[0], 0), jnp.stack(outs[1], 0), jnp.stack(outs[2], 0),
            jnp.stack(outs[3], 0), jnp.stack(outs[4], 0), jnp.stack(outs[5], 0))
```

```python
import functools
import math

import jax
import jax.numpy as jnp
from jax import lax
from jax.experimental import pallas as pl
from jax.experimental.pallas import tpu as pltpu

F32 = jnp.float32
BF16 = jnp.bfloat16
I32 = jnp.int32

RMS_EPS = 1e-6
N_ADA = 6
TOP_K = 4
CONV_WIDTH = 3
SWIGLU_ALPHA = 1.702
SWIGLU_LIMIT = 7.0
LOG2E = 1.4426950408889634

LANES = 128
SUBLANES = 8
VMEM_LIMIT = 56 * 1024 * 1024

TOKEN_TILE = 512
ATTN_TILE = 2048
ATTN_KEYS = 512
ATTN_COLS = 256
ATTN_STAGE_LAG = 2
EXPERT_BLOCK = 256
COMBINE_TILE = 256
PAGES_PER_STEP = 16
SAMPLE_PAGE_GROUP = 4
SAMPLE_STAGE_LAG = 2
ROW_UNROLL = 8


def _params(*sem):
    return pltpu.CompilerParams(dimension_semantics=sem, vmem_limit_bytes=VMEM_LIMIT)


def _split_bf16(a):
    hi = a.astype(BF16)
    lo = (a - hi.astype(F32)).astype(BF16)
    return hi, lo


def _dot3(a, w):
    a_hi, a_lo = _split_bf16(a)
    w_hi, w_lo = _split_bf16(w)
    d = functools.partial(jnp.dot, preferred_element_type=F32)
    return d(a_hi, w_hi) + (d(a_hi, w_lo) + d(a_lo, w_hi))


def _adaln_kernel(c_ref, w_ref, b_ref, o_ref):
    c = c_ref[...]
    s = c * jax.nn.sigmoid(c)
    o_ref[...] = _dot3(s, w_ref[...]) + b_ref[...]


def _adaln(c_all, w_ada, b_ada):
    rows, d = c_all.shape
    n = w_ada.shape[1]
    tn = d
    return pl.pallas_call(
        _adaln_kernel,
        grid=(n // tn,),
        in_specs=[pl.BlockSpec((rows, d), lambda j: (0, 0)),
                  pl.BlockSpec((d, tn), lambda j: (0, j)),
                  pl.BlockSpec((1, tn), lambda j: (0, j))],
        out_specs=pl.BlockSpec((rows, tn), lambda j: (0, j)),
        out_shape=jax.ShapeDtypeStruct((rows, n), F32),
        compiler_params=_params("parallel"),
        name="adaln",
    )(c_all, w_ada, b_ada.reshape(1, n))


def _premix_kernel(x_ref, shift_ref, scale_ref, g1_ref, win_ref, gq_ref, gk_ref, gmat_ref, cw_ref,
                   *rest, sample, tiles_per_batch, att, dk, q_scale):
    if sample:
        st_ref, q_out, kf_out, vf_out, yc_out, cs_out, ubuf = rest
    else:
        q_out, kf_out, vf_out, kb_out, vb_out, yc_out, cs_out, ubuf = rest
    tm = x_ref.shape[0]
    conv = yc_out.shape[1]

    x = x_ref[...]
    ms = jnp.mean(x * x, axis=-1, keepdims=True)
    xn = x * lax.rsqrt(ms + RMS_EPS) * g1_ref[...]
    h = (xn * (1.0 + scale_ref[...]) + shift_ref[...]).astype(BF16)
    z = jnp.dot(h, win_ref[...], preferred_element_type=F32)

    def group_norm(t, g_ref):
        ss = jnp.dot((t * t).astype(BF16), gmat_ref[...], preferred_element_type=F32)
        return t * lax.rsqrt(ss * (1.0 / dk) + RMS_EPS) * g_ref[...]

    q = group_norm(z[:, 0:att], gq_ref)
    k = group_norm(z[:, att:2 * att], gk_ref)
    v = z[:, 2 * att:3 * att]
    cx = z[:, 3 * att:3 * att + conv]
    cb = z[:, 3 * att + conv:3 * att + 2 * conv]
    cc = z[:, 3 * att + 2 * conv:3 * att + 3 * conv]

    q_out[...] = (q * q_scale).astype(q_out.dtype)
    nheads = kf_out.shape[0] // tm
    dv = att // nheads
    for hd in range(nheads):
        kf_out[pl.ds(hd, tm, stride=nheads), :] = k[:, hd * dv:(hd + 1) * dv]
        vf_out[pl.ds(hd, tm, stride=nheads), :] = v[:, hd * dv:(hd + 1) * dv]
    if not sample:
        kb_out[...] = k.astype(BF16)
        vb_out[...] = v.astype(BF16)

    u = cc * cx
    w0 = cw_ref[0:1, :]
    w1 = cw_ref[1:2, :]
    w2 = cw_ref[2:3, :]
    if sample:
        nseq = tm // SUBLANES
        u3 = u.reshape(nseq, SUBLANES, conv)
        ubuf[:, 8:16, :] = u3
        ubuf[:, 6:8, :] = st_ref[...]
        um1 = ubuf[:, 7:15, :]
        um2 = ubuf[:, 6:14, :]
        y3 = w0 * um2 + w1 * um1 + w2 * u3
        y = y3.reshape(tm, conv)
        cs_out[...] = ubuf[:, 14:16, :]
    else:
        i = pl.program_id(0)

        @pl.when(i % tiles_per_batch == 0)
        def _():
            ubuf[0:8, :] = jnp.zeros((8, conv), F32)

        ubuf[8:tm + 8, :] = u
        um1 = ubuf[7:tm + 7, :]
        um2 = ubuf[6:tm + 6, :]
        y = w0 * um2 + w1 * um1 + w2 * u
        ubuf[0:8, :] = ubuf[tm:tm + 8, :]
        cs_out[...] = u[tm - 2:tm, :]
    yc_out[...] = (cb * y).astype(BF16)


def _premix(x, mod_specs, mods, g1, win_bf, gq, gk, gmat, conv_w, state, *, sample, seq_len, dk):
    t, d = x.shape
    att = gq.shape[1]
    conv = conv_w.shape[1]
    ncol = win_bf.shape[1]
    tm = min(TOKEN_TILE, t)
    nt = t // tm
    tiles_per_batch = max(seq_len // tm, 1)
    q_scale = dk ** -0.5 * LOG2E
    const = lambda shape: pl.BlockSpec(shape, lambda i: (0,) * len(shape))
    row = lambda w: pl.BlockSpec((tm, w), lambda i: (i, 0))
    dv = 2 * dk
    nheads = att // dv
    cache_rows = pl.BlockSpec((tm * nheads, dv), lambda i: (i, 0))
    cache_shape = jax.ShapeDtypeStruct((t * nheads, dv), F32)
    in_specs = [row(d), mod_specs(0, tm), mod_specs(1, tm), const((1, d)), const((d, ncol)),
                const((1, att)), const((1, att)), const((att, att)), const((CONV_WIDTH, conv))]
    args = [x, mods, mods, g1, win_bf, gq, gk, gmat, conv_w]
    if sample:
        nseq = tm // SUBLANES
        in_specs.append(pl.BlockSpec((nseq, CONV_WIDTH - 1, conv), lambda i: (i, 0, 0)))
        args.append(state)
        out_specs = [row(att), cache_rows, cache_rows, row(conv),
                     pl.BlockSpec((nseq, CONV_WIDTH - 1, conv), lambda i: (i, 0, 0))]
        out_shape = [jax.ShapeDtypeStruct((t, att), F32), cache_shape, cache_shape,
                     jax.ShapeDtypeStruct((t, conv), BF16),
                     jax.ShapeDtypeStruct((t // SUBLANES, CONV_WIDTH - 1, conv), F32)]
        scratch = [pltpu.VMEM((nseq, 2 * SUBLANES, conv), F32)]
    else:
        nb = t // seq_len
        out_specs = [row(att), cache_rows, cache_rows, row(att), row(att), row(conv),
                     pl.BlockSpec((None, CONV_WIDTH - 1, conv), lambda i: (i // tiles_per_batch, 0, 0))]
        out_shape = [jax.ShapeDtypeStruct((t, att), BF16), cache_shape, cache_shape,
                     jax.ShapeDtypeStruct((t, att), BF16),
                     jax.ShapeDtypeStruct((t, att), BF16), jax.ShapeDtypeStruct((t, conv), BF16),
                     jax.ShapeDtypeStruct((nb, CONV_WIDTH - 1, conv), F32)]
        scratch = [pltpu.VMEM((tm + SUBLANES, conv), F32)]
    return pl.pallas_call(
        functools.partial(_premix_kernel, sample=sample, tiles_per_batch=tiles_per_batch,
                          att=att, dk=dk, q_scale=q_scale),
        grid=(nt,),
        in_specs=in_specs,
        out_specs=out_specs,
        out_shape=out_shape,
        scratch_shapes=scratch,
        compiler_params=_params("arbitrary"),
        name="premix_sample" if sample else "premix_prompt",
    )(*args)


def _lambda_value(lq1, lk1, lq2, lk2, lam_init):
    a = jnp.sum(lq1[...] * lk1[...], axis=-1, keepdims=True)
    b = jnp.sum(lq2[...] * lk2[...], axis=-1, keepdims=True)
    return jnp.exp(a) - jnp.exp(b) + lam_init


def _attn_kernel(qT_ref, k_ref, vT_ref, lq1, lk1, lq2, lk2, g_ref, o_ref,
                 acc1, acc2, m1, l1, m2, l2, *, dk, lam_init):
    tq = qT_ref.shape[1]
    tk = vT_ref.shape[2]
    qi = pl.program_id(2)
    qT = qT_ref[...].astype(F32)
    half = lax.broadcasted_iota(I32, qT.shape, 0) < dk
    rhs = (jnp.where(half, qT, 0.0).astype(BF16), jnp.where(half, 0.0, qT).astype(BF16))
    accs, ms, ls = (acc1, acc2), (m1, m2), (l1, l2)
    for hf in range(2):
        accs[hf][...] = jnp.zeros(accs[hf].shape, F32)
        ms[hf][...] = jnp.full(ms[hf].shape, -jnp.inf, F32)
        ls[hf][...] = jnp.zeros(ls[hf].shape, F32)

    cw = min(ATTN_COLS, tq)
    units = [(hf, cb) for hf in range(2) for cb in range(tq // cw)]

    def qk(kc, u):
        hf, cb = u
        return jnp.dot(kc, rhs[hf][:, cb * cw:(cb + 1) * cw], preferred_element_type=F32)

    def softmax(s, u, diag):
        hf, cb = u
        cols = slice(cb * cw, (cb + 1) * cw)
        if diag is not None and cb * cw < (diag + 1) * tk - 1:
            kpos = lax.broadcasted_iota(I32, s.shape, 0) + diag * tk
            qpos = lax.broadcasted_iota(I32, s.shape, 1) + cb * cw
            s = jnp.where(kpos <= qpos, s, -jnp.inf)
        m_prev = ms[hf][:, cols]
        m_new = jnp.maximum(m_prev, jnp.max(s, axis=0, keepdims=True))
        alpha = jnp.exp2(m_prev - m_new)
        p = jnp.exp2(s - m_new)
        ls[hf][:, cols] = alpha * ls[hf][:, cols] + jnp.sum(p, axis=0, keepdims=True)
        ms[hf][:, cols] = m_new
        return p.astype(BF16), alpha

    def pv(vc, pa, u):
        hf, cb = u
        cols = slice(cb * cw, (cb + 1) * cw)
        p, alpha = pa
        accs[hf][:, cols] = alpha * accs[hf][:, cols] + jnp.dot(vc, p, preferred_element_type=F32)

    def step(j, diag):
        kc = k_ref[pl.ds(pl.multiple_of(j * tk, tk), tk), :]
        vc = vT_ref[j]
        live = [u for u in units if diag is None or (u[1] + 1) * cw > diag * tk]
        n = len(live)
        s, pa = {}, {}
        lag = ATTN_STAGE_LAG
        for t in range(n + 2 * lag):
            if t < n:
                s[t] = qk(kc, live[t])
            if 0 <= t - lag < n:
                pa[t - lag] = softmax(s.pop(t - lag), live[t - lag], diag)
            if 0 <= t - 2 * lag < n:
                pv(vc, pa.pop(t - 2 * lag), live[t - 2 * lag])

    def body(j, c):
        step(j, None)
        return c

    kpt = tq // tk
    lax.fori_loop(0, qi * kpt, body, 0)
    for c in range(kpt):
        step(qi * kpt + c, c)

    lam = _lambda_value(lq1, lk1, lq2, lk2, lam_init)
    o = acc1[...] / l1[...] - lam * (acc2[...] / l2[...])
    on = o * lax.rsqrt(jnp.mean(o * o, axis=0, keepdims=True) + RMS_EPS) * (1.0 - lam_init)
    o_ref[...] = (on.T * g_ref[...]).astype(o_ref.dtype)


def _attn_prompt(qT, k_bf, vT, lams, subln_g, *, nheads, dk, lam_init):
    b, att, s = qT.shape
    dv = att // nheads
    tq = min(ATTN_TILE, s)
    nq = s // tq
    nk, tk = vT.shape[1] // nheads, vT.shape[3]
    lam_spec = pl.BlockSpec((1, dk), lambda bi, hi, qi: (0, 0))
    return pl.pallas_call(
        functools.partial(_attn_kernel, dk=dk, lam_init=lam_init),
        grid=(b, nheads, nq),
        in_specs=[pl.BlockSpec((None, dv, tq), lambda bi, hi, qi: (bi, hi, qi)),
                  pl.BlockSpec((None, s, dv), lambda bi, hi, qi: (bi, 0, hi)),
                  pl.BlockSpec((None, nk, dv, tk), lambda bi, hi, qi: (bi, hi, 0, 0)),
                  lam_spec, lam_spec, lam_spec, lam_spec,
                  pl.BlockSpec((1, dv), lambda bi, hi, qi: (0, 0))],
        out_specs=pl.BlockSpec((None, tq, dv), lambda bi, hi, qi: (bi, qi, hi)),
        out_shape=jax.ShapeDtypeStruct((b, s, att), BF16),
        scratch_shapes=[pltpu.VMEM((dv, tq), F32), pltpu.VMEM((dv, tq), F32),
                        pltpu.VMEM((1, tq), F32), pltpu.VMEM((1, tq), F32),
                        pltpu.VMEM((1, tq), F32), pltpu.VMEM((1, tq), F32)],
        compiler_params=_params("parallel", "parallel", "arbitrary"),
        name="attn_prompt",
    )(qT, k_bf, vT, *lams, subln_g)


def _sattn_kernel(pt_ref, q_ref, kn_ref, vn_ref, *rest, pps, nheads, dk, lam_init):
    k_refs = rest[:pps]
    v_refs = rest[pps:2 * pps]
    lq1, lk1, lq2, lk2, g_ref, o_ref, m_sc, l_sc, acc_sc = rest[2 * pps:]
    j = pl.program_id(1)
    nt = q_ref.shape[0]
    dv = 2 * dk
    prow = k_refs[0].shape[0]
    rows = 2 * nheads * nt
    nt_dims = (((1,), (1,)), ((), ()))
    head_shift = (2 * nt).bit_length() - 1

    q8 = q_ref[...]
    first = lax.broadcasted_iota(I32, (nt, dv), 1) < dk
    pieces = []
    for h in range(nheads):
        qh = q8[:, h * dv:(h + 1) * dv]
        pieces += [jnp.where(first, qh, 0.0), jnp.where(first, 0.0, qh)]
    qall = jnp.concatenate(pieces, axis=0).astype(BF16)

    @pl.when(j == 0)
    def _():
        m_sc[...] = jnp.full(m_sc.shape, -jnp.inf, F32)
        l_sc[...] = jnp.zeros(l_sc.shape, F32)
        acc_sc[...] = jnp.zeros(acc_sc.shape, F32)

    def soft(s):
        m = jnp.max(s, axis=1, keepdims=True)
        p = jnp.exp2(s - m).astype(BF16)
        return m, jnp.sum(p.astype(F32), axis=1, keepdims=True), p

    def weigh(p, vals, width):
        pv = None
        for i, v in enumerate(vals):
            d = jnp.dot(p[:, i * width:(i + 1) * width], v, preferred_element_type=F32)
            pv = d if pv is None else pv + d
        return pv

    def merge(parts):
        m = parts[0][0]
        for part in parts[1:]:
            m = jnp.maximum(m, part[0])
        l = acc = None
        for m_p, l_p, acc_p in parts:
            w = jnp.exp2(m_p - m)
            l = w * l_p if l is None else l + w * l_p
            acc = w * acc_p if acc is None else acc + w * acc_p
        return m, l, acc

    row_head = lax.broadcasted_iota(I32, (rows, prow), 0) >> head_shift
    col_head = lax.broadcasted_iota(I32, (rows, prow), 1) & (nheads - 1)
    own = row_head == col_head
    groups = [range(g0, min(g0 + SAMPLE_PAGE_GROUP, pps)) for g0 in range(0, pps, SAMPLE_PAGE_GROUP)]
    parts = [(m_sc[...], l_sc[...], acc_sc[...])]
    n, lag = len(groups), SAMPLE_STAGE_LAG
    s_of, soft_of = {}, {}
    for t in range(n + 2 * lag):
        if t < n:
            s_of[t] = jnp.concatenate(
                [jnp.where(own, lax.dot_general(qall, k_refs[i][...].astype(BF16), nt_dims,
                                                preferred_element_type=F32), -jnp.inf)
                 for i in groups[t]], axis=1)
        if 0 <= t - lag < n:
            soft_of[t - lag] = soft(s_of.pop(t - lag))
        if 0 <= t - 2 * lag < n:
            m_g, l_g, p_g = soft_of.pop(t - 2 * lag)
            parts.append((m_g, l_g, weigh(p_g, [v_refs[i][...].astype(BF16) for i in groups[t - 2 * lag]],
                                          prow)))
    state = merge(parts)
    m_sc[...], l_sc[...], acc_sc[...] = state

    @pl.when(j == pl.num_programs(1) - 1)
    def _():
        nnew = nheads * nt
        zpad = jnp.zeros((LANES - nnew, dv), F32)
        kn = jnp.concatenate([kn_ref[...], zpad], axis=0).astype(BF16)
        vn = jnp.concatenate([vn_ref[...], zpad], axis=0).astype(BF16)
        sn = lax.dot_general(qall, kn, nt_dims, preferred_element_type=F32)
        r = lax.broadcasted_iota(I32, sn.shape, 0)
        c = lax.broadcasted_iota(I32, sn.shape, 1)
        vis = jnp.logical_and((r >> head_shift) == (c & (nheads - 1)),
                              (c >> (nheads.bit_length() - 1)) <= (r & (nt - 1)))
        vis = jnp.logical_and(vis, c < nnew)
        m_n, l_n, p_n = soft(jnp.where(vis, sn, -jnp.inf))
        _, l_fin, acc_fin = merge([state, (m_n, l_n, weigh(p_n, [vn], LANES))])

        lam = _lambda_value(lq1, lk1, lq2, lk2, lam_init)
        o = acc_fin / l_fin
        for h in range(nheads):
            r0 = 2 * h * nt
            d = o[r0:r0 + nt, :] - lam * o[r0 + nt:r0 + 2 * nt, :]
            dn = d * lax.rsqrt(jnp.mean(d * d, axis=-1, keepdims=True) + RMS_EPS)
            o_ref[:, h * dv:(h + 1) * dv] = dn * g_ref[...] * (1.0 - lam_init)


def _attn_sample(q, k_new, v_new, cache_k, cache_v, page_table, lams, subln_g, *,
                 nheads, dk, dec_seq, lam_init):
    t, att = q.shape
    nseq = t // dec_seq
    dv = 2 * dk
    npool, page = cache_k.shape[0], cache_k.shape[1]
    npages = page_table.shape[1]
    pps = min(PAGES_PER_STEP, npages)
    nsteps = npages // pps
    prow = page * nheads
    assert nheads & (nheads - 1) == 0 and nheads * dec_seq <= LANES
    ck = cache_k.reshape(npool, prow, dv)
    cv = cache_v.reshape(npool, prow, dv)
    tok_spec = pl.BlockSpec((dec_seq, att), lambda s, j, pt: (s, 0))
    new_spec = pl.BlockSpec((dec_seq * nheads, dv), lambda s, j, pt: (s, 0))
    lam_spec = pl.BlockSpec((1, dk), lambda s, j, pt: (0, 0))

    def page_spec(i):
        return pl.BlockSpec((None, prow, dv),
                            lambda s, j, pt: (pt[s * npages + j * pps + i], 0, 0))

    rows = 2 * nheads * dec_seq
    grid_spec = pltpu.PrefetchScalarGridSpec(
        num_scalar_prefetch=1,
        grid=(nseq, nsteps),
        in_specs=[tok_spec, new_spec, new_spec]
                 + [page_spec(i) for i in range(pps)] + [page_spec(i) for i in range(pps)]
                 + [lam_spec] * 4 + [pl.BlockSpec((1, dv), lambda s, j, pt: (0, 0))],
        out_specs=tok_spec,
        scratch_shapes=[pltpu.VMEM((rows, 1), F32), pltpu.VMEM((rows, 1), F32),
                        pltpu.VMEM((rows, dv), F32)],
    )
    return pl.pallas_call(
        functools.partial(_sattn_kernel, pps=pps, nheads=nheads, dk=dk, lam_init=lam_init),
        grid_spec=grid_spec,
        out_shape=jax.ShapeDtypeStruct((t, att), F32),
        compiler_params=_params("parallel", "arbitrary"),
        name="attn_sample",
    )(page_table.reshape(-1), q, k_new, v_new, *([ck] * pps), *([cv] * pps), *lams, subln_g)


def _post_kernel(x_ref, a_ref, yc_ref, gate_ref, shift_ref, scale_ref, woa_ref, woc_ref, g2_ref,
                 wr_ref, br_ref, tri_ref, x1_out, h2_out, ri_out, ptok_out, cnt_out, carry_sc, *, ne):
    i = pl.program_id(0)
    tm = x_ref.shape[0]

    @pl.when(i == 0)
    def _():
        carry_sc[...] = jnp.zeros(carry_sc.shape, F32)

    mix = (jnp.dot(a_ref[...].astype(BF16), woa_ref[...], preferred_element_type=F32)
           + jnp.dot(yc_ref[...], woc_ref[...], preferred_element_type=F32))
    x1 = x_ref[...] + gate_ref[...] * mix
    ms = jnp.mean(x1 * x1, axis=-1, keepdims=True)
    h2 = (x1 * lax.rsqrt(ms + RMS_EPS) * g2_ref[...]) * (1.0 + scale_ref[...]) + shift_ref[...]
    x1_out[...] = x1
    h2_out[...] = h2

    logits = _dot3(h2, wr_ref[...]) + br_ref[...]
    lt = logits.T[0:ne, :]
    eidx = lax.broadcasted_iota(I32, lt.shape, 0)
    vals, idxs = [], []
    for _ in range(TOP_K):
        mx = jnp.max(lt, axis=0, keepdims=True)
        ik = jnp.min(jnp.where(lt == mx, eidx, ne), axis=0, keepdims=True)
        vals.append(mx)
        idxs.append(ik)
        lt = jnp.where(eidx == ik, -jnp.inf, lt)
    ex = [jnp.exp(v - vals[0]) for v in vals]
    den = ex[0] + ex[1] + ex[2] + ex[3]
    probs = [e / den for e in ex]

    sel = jnp.zeros(lt.shape, F32)
    for ik in idxs:
        sel = sel + jnp.where(eidx == ik, 1.0, 0.0)
    cum = jnp.dot(sel.astype(BF16), tri_ref[...], preferred_element_type=F32) + carry_sc[...]
    ranks = [jnp.sum(jnp.where(eidx == ik, cum, 0.0), axis=0, keepdims=True) for ik in idxs]
    carry_new = carry_sc[...] + jnp.sum(sel, axis=1, keepdims=True)
    carry_sc[...] = carry_new
    cnt_out[...] = jnp.broadcast_to(carry_new, cnt_out.shape)

    ri_out[...] = jnp.concatenate(idxs + [r.astype(I32) for r in ranks], axis=0)
    ppad = jnp.concatenate(probs + [jnp.zeros((LANES - TOP_K, tm), F32)], axis=0)
    ptok_out[...] = ppad.T


def _post(x, a, yc, mod_specs, mods, woa, woc, g2, wr_pad, br_pad, *, seq_len, ne):
    t, d = x.shape
    att = a.shape[1]
    conv = yc.shape[1]
    tm = min(TOKEN_TILE, t)
    nt = t // tm
    tri = (lax.broadcasted_iota(I32, (tm, tm), 0) < lax.broadcasted_iota(I32, (tm, tm), 1)).astype(BF16)
    const = lambda shape: pl.BlockSpec(shape, lambda i: (0,) * len(shape))
    row = lambda w: pl.BlockSpec((tm, w), lambda i: (i, 0))
    return pl.pallas_call(
        functools.partial(_post_kernel, ne=ne),
        grid=(nt,),
        in_specs=[row(d), row(att), row(conv), mod_specs(2, tm), mod_specs(3, tm), mod_specs(4, tm),
                  const((att, d)), const((conv, d)), const((1, d)), const((d, LANES)), const((1, LANES)),
                  const((tm, tm))],
        out_specs=[row(d), row(d), pl.BlockSpec((2 * TOP_K, tm), lambda i: (0, i)), row(LANES),
                   const((ne, LANES))],
        out_shape=[jax.ShapeDtypeStruct((t, d), F32), jax.ShapeDtypeStruct((t, d), F32),
                   jax.ShapeDtypeStruct((2 * TOP_K, t), I32), jax.ShapeDtypeStruct((t, LANES), F32),
                   jax.ShapeDtypeStruct((ne, LANES), F32)],
        scratch_shapes=[pltpu.VMEM((ne, 1), F32)],
        compiler_params=_params("arbitrary"),
        name="post_mix",
    )(x, a, yc, mods, mods, mods, woa, woc, g2, wr_pad, br_pad, tri)


def _dispatch_kernel(last_ref, padded_ref, nu_ref, dest_hbm, h_ref, xs_hbm, idx_smem, zbuf, sem, *, tmd):
    i = pl.program_id(0)
    bm = zbuf.shape[0]
    ne = last_ref.shape[0]
    nblk = xs_hbm.shape[0] // bm

    @pl.when(i == 0)
    def _():
        zbuf[...] = jnp.zeros(zbuf.shape, zbuf.dtype)

        def zero_block(b):
            return pltpu.make_async_copy(zbuf, xs_hbm.at[pl.ds(pl.multiple_of(b * bm, bm), bm)], sem.at[2])

        def tail_start(b, c):
            zero_block(b).start()
            return c

        def tail_wait(b, c):
            zero_block(b).wait()
            return c

        for e in range(ne):
            @pl.when(padded_ref[e] > 0)
            def _():
                zero_block(last_ref[e]).start()
        lax.fori_loop(nu_ref[0], nblk, tail_start, 0)
        for e in range(ne):
            @pl.when(padded_ref[e] > 0)
            def _():
                zero_block(last_ref[e]).wait()
        lax.fori_loop(nu_ref[0], nblk, tail_wait, 0)

    cp = pltpu.make_async_copy(dest_hbm.at[i], idx_smem, sem.at[0])
    cp.start()
    cp.wait()

    def issue(g, c):
        for u in range(ROW_UNROLL):
            t = g * ROW_UNROLL + u
            for k in range(TOP_K):
                d = idx_smem[k * tmd + t]
                pltpu.make_async_copy(h_ref.at[pl.ds(t, 1)], xs_hbm.at[pl.ds(d, 1)], sem.at[1]).start()
        return c

    lax.fori_loop(0, tmd // ROW_UNROLL, issue, 0)
    for k in range(TOP_K):
        pltpu.make_async_copy(h_ref, xs_hbm.at[pl.ds(0, tmd)], sem.at[1]).wait()


def _dispatch(last_blk, padded, n_used, dest_tiles, h2, n_rows):
    t, d = h2.shape
    nt, width = dest_tiles.shape
    tmd = width // TOP_K
    grid_spec = pltpu.PrefetchScalarGridSpec(
        num_scalar_prefetch=3,
        grid=(nt,),
        in_specs=[pl.BlockSpec(memory_space=pl.ANY), pl.BlockSpec((tmd, d), lambda i, *_: (i, 0))],
        out_specs=pl.BlockSpec(memory_space=pl.ANY),
        scratch_shapes=[pltpu.SMEM((width,), I32), pltpu.VMEM((EXPERT_BLOCK, d), h2.dtype),
                        pltpu.SemaphoreType.DMA((3,))],
    )
    return pl.pallas_call(
        functools.partial(_dispatch_kernel, tmd=tmd),
        grid_spec=grid_spec,
        out_shape=jax.ShapeDtypeStruct((n_rows, d), h2.dtype),
        compiler_params=_params("arbitrary"),
        name="moe_dispatch",
    )(last_blk, padded, n_used, dest_tiles, h2)


def _expert_kernel(be_ref, nu_ref, x_ref, wg_ref, bg_ref, wu_ref, bu_ref, wd_ref, bd_ref, y_ref,
                   wg_sc, wu_sc, wd_sc):
    i = pl.program_id(0)
    used = i < nu_ref[0]
    prev = be_ref[jnp.maximum(i - 1, 0)]
    fresh = jnp.logical_or(i == 0, be_ref[i] != prev)

    @pl.when(jnp.logical_and(used, fresh))
    def _():
        wg_sc[...] = wg_ref[...].astype(BF16)
        wu_sc[...] = wu_ref[...].astype(BF16)
        wd_sc[...] = wd_ref[...].astype(BF16)

    @pl.when(used)
    def _():
        x = x_ref[...].astype(BF16)
        g = jnp.minimum(jnp.dot(x, wg_sc[...], preferred_element_type=F32) + bg_ref[...], SWIGLU_LIMIT)
        u = jnp.clip(jnp.dot(x, wu_sc[...], preferred_element_type=F32) + bu_ref[...],
                     -SWIGLU_LIMIT, SWIGLU_LIMIT)
        a = g * jax.nn.sigmoid(SWIGLU_ALPHA * g) * (u + 1.0)
        y_ref[...] = jnp.dot(a.astype(BF16), wd_sc[...], preferred_element_type=F32) + bd_ref[...]

    @pl.when(jnp.logical_not(used))
    def _():
        y_ref[...] = jnp.zeros(y_ref.shape, F32)


def _experts(block_e, n_used, xs, w_gate, b_gate, w_up, b_up, w_down, b_down):
    n_rows, d = xs.shape
    ne, _, dff = w_gate.shape
    bm = EXPERT_BLOCK
    nblk = n_rows // bm

    def blk(i, be, nu):
        return jnp.minimum(i, nu[0] - 1)

    row_spec = pl.BlockSpec((bm, d), lambda i, be, nu: (blk(i, be, nu), 0))
    w_spec = lambda k, n: pl.BlockSpec((None, k, n), lambda i, be, nu: (be[blk(i, be, nu)], 0, 0))
    grid_spec = pltpu.PrefetchScalarGridSpec(
        num_scalar_prefetch=2,
        grid=(nblk,),
        in_specs=[row_spec, w_spec(d, dff), w_spec(1, dff), w_spec(d, dff), w_spec(1, dff),
                  w_spec(dff, d), w_spec(1, d)],
        out_specs=pl.BlockSpec((bm, d), lambda i, be, nu: (i, 0)),
        scratch_shapes=[pltpu.VMEM((d, dff), BF16), pltpu.VMEM((d, dff), BF16), pltpu.VMEM((dff, d), BF16)],
    )
    return pl.pallas_call(
        _expert_kernel,
        grid_spec=grid_spec,
        out_shape=jax.ShapeDtypeStruct((n_rows, d), F32),
        compiler_params=_params("arbitrary"),
        name="moe_experts",
    )(block_e, n_used, xs, w_gate, b_gate.reshape(ne, 1, dff), w_up, b_up.reshape(ne, 1, dff),
      w_down, b_down.reshape(ne, 1, d))


def _combine_kernel(dest_hbm, x1_ref, gate_ref, p_ref, yb_hbm, o_ref, idx_smem, buf, sem, *, tmc):
    i = pl.program_id(0)
    cp = pltpu.make_async_copy(dest_hbm.at[i], idx_smem, sem.at[0])
    cp.start()
    cp.wait()

    def issue(g, c):
        for u in range(ROW_UNROLL):
            t = g * ROW_UNROLL + u
            for k in range(TOP_K):
                d = idx_smem[k * tmc + t]
                pltpu.make_async_copy(yb_hbm.at[pl.ds(d, 1)], buf.at[k, pl.ds(t, 1)], sem.at[1]).start()
        return c

    lax.fori_loop(0, tmc // ROW_UNROLL, issue, 0)
    for k in range(TOP_K):
        pltpu.make_async_copy(yb_hbm.at[pl.ds(0, tmc)], buf.at[k], sem.at[1]).wait()

    p = p_ref[...]
    y = p[:, 0:1] * buf[0]
    for k in range(1, TOP_K):
        y = y + p[:, k:k + 1] * buf[k]
    o_ref[...] = x1_ref[...] + gate_ref[...] * y


def _combine(dest_tiles, x1, mod_specs, mods, ptok, yb):
    t, d = x1.shape
    nt, width = dest_tiles.shape
    tmc = width // TOP_K
    row = lambda w: pl.BlockSpec((tmc, w), lambda i: (i, 0))
    return pl.pallas_call(
        functools.partial(_combine_kernel, tmc=tmc),
        grid=(nt,),
        in_specs=[pl.BlockSpec(memory_space=pl.ANY), row(d), mod_specs(5, tmc), row(LANES),
                  pl.BlockSpec(memory_space=pl.ANY)],
        out_specs=row(d),
        out_shape=jax.ShapeDtypeStruct((t, d), F32),
        scratch_shapes=[pltpu.SMEM((width,), I32), pltpu.VMEM((TOP_K, tmc, d), F32),
                        pltpu.SemaphoreType.DMA((2,))],
        compiler_params=_params("arbitrary"),
        name="moe_combine",
    )(dest_tiles, x1, mods, ptok, yb)


def _dest_tiles(dest, tile):
    t = dest.shape[1]
    return dest.reshape(TOP_K, t // tile, tile).transpose(1, 0, 2).reshape(t // tile, TOP_K * tile)


def _moe(h2, x1, ri, ptok, cnt, mod_specs, mods, w_gate, b_gate, w_up, b_up, w_down, b_down):
    t, d = h2.shape
    ne = w_gate.shape[0]
    bm = EXPERT_BLOCK
    n_rows = -(-(t * TOP_K) // bm) * bm + ne * bm
    counts = cnt[:, 0].astype(I32)
    padded = (counts + bm - 1) // bm * bm
    pend = jnp.cumsum(padded)
    pstart = pend - padded
    eids = jnp.arange(ne, dtype=I32)
    dest = ri[TOP_K:2 * TOP_K] + jnp.sum(
        jnp.where(ri[0:TOP_K, :, None] == eids, pstart, 0), axis=-1)
    nblk = n_rows // bm
    first_row = jnp.arange(nblk, dtype=I32) * bm
    block_e = jnp.minimum(jnp.sum((pend[None, :] <= first_row[:, None]).astype(I32), axis=1), ne - 1)
    n_used = (pend[-1:] // bm).astype(I32)
    tmd = min(TOKEN_TILE, t)
    tmc = min(COMBINE_TILE, t)
    last_blk = (pend // bm - 1).astype(I32)
    xs = _dispatch(last_blk, padded.astype(I32), n_used, _dest_tiles(dest, tmd), h2, n_rows)
    yb = _experts(block_e, n_used, xs, w_gate, b_gate, w_up, b_up, w_down, b_down)
    return _combine(_dest_tiles(dest, tmc), x1, mod_specs, mods, ptok, yb)


def kernel(x_prompt, x_sample, cache_k, cache_v, state_conv, page_table, c_prompt, c_sample,
           w_ada, b_ada, norm1_g, w_in, q_norm_g, k_norm_g, lambda_q1, lambda_k1, lambda_q2,
           lambda_k2, subln_g, conv_w, w_out, norm2_g, w_router, b_router, w_gate, b_gate,
           w_up, b_up, w_down, b_down):
    nb, seq, d = x_prompt.shape
    db, dec_seq, _ = x_sample.shape
    depth, _, page, nheads, dv = cache_k.shape
    dk = dv // 2
    att = nheads * dv
    conv = conv_w.shape[2]
    ne = w_router.shape[2]
    tp, ts = nb * seq, db * dec_seq
    assert dec_seq == SUBLANES and att % LANES == 0 and conv % LANES == 0 and ne <= LANES
    assert dk & (dk - 1) == 0

    c_rows = -(-(nb + db) // SUBLANES) * SUBLANES
    c_all = jnp.concatenate([c_prompt, c_sample, jnp.zeros((c_rows - nb - db, d), F32)], axis=0)
    gidx = jnp.arange(att, dtype=I32) // dk
    gmat = (gidx[:, None] == gidx[None, :]).astype(BF16)

    yp = x_prompt.reshape(tp, d)
    ys = x_sample.reshape(ts, d)
    outs = [[] for _ in range(6)]
    for l in range(depth):
        lam_init = 0.8 - 0.6 * math.exp(-0.3 * l)
        mod_all = _adaln(c_all, w_ada[l], b_ada[l])
        mod_p = mod_all[:nb].reshape(nb, N_ADA, 1, d)
        mod_s = jnp.repeat(mod_all[nb:nb + db], dec_seq, axis=0)

        def mod_specs_p(j, tile):
            per_batch = max(seq // tile, 1)
            return pl.BlockSpec((None, None, 1, d), lambda i: (i // per_batch, j, 0, 0))

        def mod_specs_s(j, tile):
            return pl.BlockSpec((tile, d), lambda i: (i, j))

        win_bf = w_in[l].astype(BF16)
        woa = w_out[l][:att].astype(BF16)
        woc = w_out[l][att:].astype(BF16)
        g1 = norm1_g[l].reshape(1, d)
        g2 = norm2_g[l].reshape(1, d)
        gq = jnp.tile(q_norm_g[l], att // dk).reshape(1, att)
        gk = jnp.tile(k_norm_g[l], att // dk).reshape(1, att)
        lams = [v[l].reshape(1, dk) for v in (lambda_q1, lambda_k1, lambda_q2, lambda_k2)]
        sg = subln_g[l].reshape(1, dv)
        wr_pad = jnp.pad(w_router[l], ((0, 0), (0, LANES - ne)))
        br_pad = jnp.pad(b_router[l], (0, LANES - ne)).reshape(1, LANES)
        moe_w = (w_gate[l], b_gate[l], w_up[l], b_up[l], w_down[l], b_down[l])

        q_bf, k_f, v_f, k_bf, v_bf, yc, conv_p = _premix(
            yp, mod_specs_p, mod_p, g1, win_bf, gq, gk, gmat, conv_w[l], None,
            sample=False, seq_len=seq, dk=dk)
        tk = min(ATTN_KEYS, seq)
        nk = seq // tk
        qT = q_bf.reshape(nb, seq, att).transpose(0, 2, 1)
        vT = (v_bf.reshape(nb, nk, tk, nheads, dv).transpose(0, 3, 1, 4, 2)
              .reshape(nb, nheads * nk, dv, tk))
        a_p = _attn_prompt(qT, k_bf.reshape(nb, seq, att), vT, lams, sg,
                           nheads=nheads, dk=dk, lam_init=lam_init).reshape(tp, att)
        x1, h2, ri, ptok, cnt = _post(yp, a_p, yc, mod_specs_p, mod_p, woa, woc, g2, wr_pad, br_pad,
                                      seq_len=seq, ne=ne)
        yp = _moe(h2, x1, ri, ptok, cnt, mod_specs_p, mod_p, *moe_w)
        outs[0].append(k_f.reshape(nb, seq, nheads, dv))
        outs[1].append(v_f.reshape(nb, seq, nheads, dv))
        outs[2].append(conv_p)

        q_s, k_s, v_s, yc_s, conv_s = _premix(
            ys, mod_specs_s, mod_s, g1, win_bf, gq, gk, gmat, conv_w[l], state_conv[l],
            sample=True, seq_len=dec_seq, dk=dk)
        a_s = _attn_sample(q_s, k_s, v_s, cache_k[l], cache_v[l], page_table, lams, sg,
                           nheads=nheads, dk=dk, dec_seq=dec_seq, lam_init=lam_init)
        x1, h2, ri, ptok, cnt = _post(ys, a_s, yc_s, mod_specs_s, mod_s, woa, woc, g2, wr_pad, br_pad,
                                      seq_len=dec_seq, ne=ne)
        ys = _moe(h2, x1, ri, ptok, cnt, mod_specs_s, mod_s, *moe_w)
        outs[3].append(k_s.reshape(db, dec_seq, nheads, dv))
        outs[4].append(v_s.reshape(db, dec_seq, nheads, dv))
        outs[5].append(conv_s)

    return (yp.reshape(nb, seq, d), ys.reshape(db, dec_seq, d),
            jnp.stack(outs[0], 0), jnp.stack(outs[1], 0), jnp.stack(outs[2], 0),
            jnp.stack(outs[3], 0), jnp.stack(outs[4], 0), jnp.stack(outs[5], 0))
```

```python
import functools
import math

import jax
import jax.numpy as jnp
from jax import lax
from jax.experimental import pallas as pl
from jax.experimental.pallas import tpu as pltpu

F32 = jnp.float32
BF16 = jnp.bfloat16
I32 = jnp.int32

RMS_EPS = 1e-6
N_ADA = 6
TOP_K = 4
CONV_WIDTH = 3
SWIGLU_ALPHA = 1.702
SWIGLU_LIMIT = 7.0
LOG2E = 1.4426950408889634

LANES = 128
SUBLANES = 8
VMEM_LIMIT = 56 * 1024 * 1024

TOKEN_TILE = 512
ATTN_TILE = 2048
ATTN_KEYS = 512
ATTN_COLS = 256
ATTN_STAGE_LAG = 2
EXPERT_BLOCK = 256
COMBINE_TILE = 256
PAGES_PER_STEP = 16
SAMPLE_PAGE_GROUP = 4
SAMPLE_STAGE_LAG = 2
ROW_UNROLL = 8
MAX_SMEM_DEST = 128 * 1024


def _params(*sem):
    return pltpu.CompilerParams(dimension_semantics=sem, vmem_limit_bytes=VMEM_LIMIT)


def _split_bf16(a):
    hi = a.astype(BF16)
    lo = (a - hi.astype(F32)).astype(BF16)
    return hi, lo


def _dot3(a, w):
    a_hi, a_lo = _split_bf16(a)
    w_hi, w_lo = _split_bf16(w)
    d = functools.partial(jnp.dot, preferred_element_type=F32)
    return d(a_hi, w_hi) + (d(a_hi, w_lo) + d(a_lo, w_hi))


def _adaln_kernel(c_ref, w_ref, b_ref, o_ref):
    c = c_ref[...]
    s = c * jax.nn.sigmoid(c)
    o_ref[...] = _dot3(s, w_ref[...]) + b_ref[...]


def _adaln(c_all, w_ada, b_ada):
    rows, d = c_all.shape
    n = w_ada.shape[1]
    tn = d
    return pl.pallas_call(
        _adaln_kernel,
        grid=(n // tn,),
        in_specs=[pl.BlockSpec((rows, d), lambda j: (0, 0)),
                  pl.BlockSpec((d, tn), lambda j: (0, j)),
                  pl.BlockSpec((1, tn), lambda j: (0, j))],
        out_specs=pl.BlockSpec((rows, tn), lambda j: (0, j)),
        out_shape=jax.ShapeDtypeStruct((rows, n), F32),
        compiler_params=_params("parallel"),
        name="adaln",
    )(c_all, w_ada, b_ada.reshape(1, n))


def _premix_kernel(x_ref, shift_ref, scale_ref, g1_ref, win_ref, gq_ref, gk_ref, gmat_ref, cw_ref,
                   *rest, sample, tiles_per_batch, att, dk, q_scale):
    if sample:
        st_ref, q_out, kf_out, vf_out, yc_out, cs_out, ubuf = rest
    else:
        q_out, kf_out, vf_out, kb_out, vb_out, yc_out, cs_out, ubuf = rest
    tm = x_ref.shape[0]
    conv = yc_out.shape[1]

    x = x_ref[...]
    ms = jnp.mean(x * x, axis=-1, keepdims=True)
    xn = x * lax.rsqrt(ms + RMS_EPS) * g1_ref[...]
    h = (xn * (1.0 + scale_ref[...]) + shift_ref[...]).astype(BF16)
    z = jnp.dot(h, win_ref[...], preferred_element_type=F32)

    def group_norm(t, g_ref):
        ss = jnp.dot((t * t).astype(BF16), gmat_ref[...], preferred_element_type=F32)
        return t * lax.rsqrt(ss * (1.0 / dk) + RMS_EPS) * g_ref[...]

    q = group_norm(z[:, 0:att], gq_ref)
    k = group_norm(z[:, att:2 * att], gk_ref)
    v = z[:, 2 * att:3 * att]
    cx = z[:, 3 * att:3 * att + conv]
    cb = z[:, 3 * att + conv:3 * att + 2 * conv]
    cc = z[:, 3 * att + 2 * conv:3 * att + 3 * conv]

    q_out[...] = (q * q_scale).astype(q_out.dtype)
    nheads = kf_out.shape[0] // tm
    dv = att // nheads
    for hd in range(nheads):
        kf_out[pl.ds(hd, tm, stride=nheads), :] = k[:, hd * dv:(hd + 1) * dv]
        vf_out[pl.ds(hd, tm, stride=nheads), :] = v[:, hd * dv:(hd + 1) * dv]
    if not sample:
        kb_out[...] = k.astype(BF16)
        vb_out[...] = v.astype(BF16)

    u = cc * cx
    w0 = cw_ref[0:1, :]
    w1 = cw_ref[1:2, :]
    w2 = cw_ref[2:3, :]
    if sample:
        nseq = tm // SUBLANES
        u3 = u.reshape(nseq, SUBLANES, conv)
        ubuf[:, 8:16, :] = u3
        ubuf[:, 6:8, :] = st_ref[...]
        um1 = ubuf[:, 7:15, :]
        um2 = ubuf[:, 6:14, :]
        y3 = w0 * um2 + w1 * um1 + w2 * u3
        y = y3.reshape(tm, conv)
        cs_out[...] = ubuf[:, 14:16, :]
    else:
        i = pl.program_id(0)

        @pl.when(i % tiles_per_batch == 0)
        def _():
            ubuf[0:8, :] = jnp.zeros((8, conv), F32)

        ubuf[8:tm + 8, :] = u
        um1 = ubuf[7:tm + 7, :]
        um2 = ubuf[6:tm + 6, :]
        y = w0 * um2 + w1 * um1 + w2 * u
        ubuf[0:8, :] = ubuf[tm:tm + 8, :]
        cs_out[...] = u[tm - 2:tm, :]
    yc_out[...] = (cb * y).astype(BF16)


def _premix(x, mod_specs, mods, g1, win_bf, gq, gk, gmat, conv_w, state, *, sample, seq_len, dk):
    t, d = x.shape
    att = gq.shape[1]
    conv = conv_w.shape[1]
    ncol = win_bf.shape[1]
    tm = min(TOKEN_TILE, t)
    nt = t // tm
    tiles_per_batch = max(seq_len // tm, 1)
    q_scale = dk ** -0.5 * LOG2E
    const = lambda shape: pl.BlockSpec(shape, lambda i: (0,) * len(shape))
    row = lambda w: pl.BlockSpec((tm, w), lambda i: (i, 0))
    dv = 2 * dk
    nheads = att // dv
    cache_rows = pl.BlockSpec((tm * nheads, dv), lambda i: (i, 0))
    cache_shape = jax.ShapeDtypeStruct((t * nheads, dv), F32)
    in_specs = [row(d), mod_specs(0, tm), mod_specs(1, tm), const((1, d)), const((d, ncol)),
                const((1, att)), const((1, att)), const((att, att)), const((CONV_WIDTH, conv))]
    args = [x, mods, mods, g1, win_bf, gq, gk, gmat, conv_w]
    if sample:
        nseq = tm // SUBLANES
        in_specs.append(pl.BlockSpec((nseq, CONV_WIDTH - 1, conv), lambda i: (i, 0, 0)))
        args.append(state)
        out_specs = [row(att), cache_rows, cache_rows, row(conv),
                     pl.BlockSpec((nseq, CONV_WIDTH - 1, conv), lambda i: (i, 0, 0))]
        out_shape = [jax.ShapeDtypeStruct((t, att), F32), cache_shape, cache_shape,
                     jax.ShapeDtypeStruct((t, conv), BF16),
                     jax.ShapeDtypeStruct((t // SUBLANES, CONV_WIDTH - 1, conv), F32)]
        scratch = [pltpu.VMEM((nseq, 2 * SUBLANES, conv), F32)]
    else:
        nb = t // seq_len
        out_specs = [row(att), cache_rows, cache_rows, row(att), row(att), row(conv),
                     pl.BlockSpec((None, CONV_WIDTH - 1, conv), lambda i: (i // tiles_per_batch, 0, 0))]
        out_shape = [jax.ShapeDtypeStruct((t, att), BF16), cache_shape, cache_shape,
                     jax.ShapeDtypeStruct((t, att), BF16),
                     jax.ShapeDtypeStruct((t, att), BF16), jax.ShapeDtypeStruct((t, conv), BF16),
                     jax.ShapeDtypeStruct((nb, CONV_WIDTH - 1, conv), F32)]
        scratch = [pltpu.VMEM((tm + SUBLANES, conv), F32)]
    return pl.pallas_call(
        functools.partial(_premix_kernel, sample=sample, tiles_per_batch=tiles_per_batch,
                          att=att, dk=dk, q_scale=q_scale),
        grid=(nt,),
        in_specs=in_specs,
        out_specs=out_specs,
        out_shape=out_shape,
        scratch_shapes=scratch,
        compiler_params=_params("arbitrary"),
        name="premix_sample" if sample else "premix_prompt",
    )(*args)


def _lambda_value(lq1, lk1, lq2, lk2, lam_init):
    a = jnp.sum(lq1[...] * lk1[...], axis=-1, keepdims=True)
    b = jnp.sum(lq2[...] * lk2[...], axis=-1, keepdims=True)
    return jnp.exp(a) - jnp.exp(b) + lam_init


def _attn_kernel(qT_ref, k_ref, vT_ref, lq1, lk1, lq2, lk2, g_ref, o_ref,
                 acc1, acc2, m1, l1, m2, l2, *, dk, lam_init):
    tq = qT_ref.shape[1]
    tk = vT_ref.shape[2]
    qi = pl.program_id(2)
    qT = qT_ref[...].astype(F32)
    half = lax.broadcasted_iota(I32, qT.shape, 0) < dk
    rhs = (jnp.where(half, qT, 0.0).astype(BF16), jnp.where(half, 0.0, qT).astype(BF16))
    accs, ms, ls = (acc1, acc2), (m1, m2), (l1, l2)
    for hf in range(2):
        accs[hf][...] = jnp.zeros(accs[hf].shape, F32)
        ms[hf][...] = jnp.full(ms[hf].shape, -jnp.inf, F32)
        ls[hf][...] = jnp.zeros(ls[hf].shape, F32)

    cw = min(ATTN_COLS, tq)
    units = [(hf, cb) for hf in range(2) for cb in range(tq // cw)]

    def qk(kc, u):
        hf, cb = u
        return jnp.dot(kc, rhs[hf][:, cb * cw:(cb + 1) * cw], preferred_element_type=F32)

    def softmax(s, u, diag):
        hf, cb = u
        cols = slice(cb * cw, (cb + 1) * cw)
        if diag is not None and cb * cw < (diag + 1) * tk - 1:
            kpos = lax.broadcasted_iota(I32, s.shape, 0) + diag * tk
            qpos = lax.broadcasted_iota(I32, s.shape, 1) + cb * cw
            s = jnp.where(kpos <= qpos, s, -jnp.inf)
        m_prev = ms[hf][:, cols]
        m_new = jnp.maximum(m_prev, jnp.max(s, axis=0, keepdims=True))
        alpha = jnp.exp2(m_prev - m_new)
        p = jnp.exp2(s - m_new)
        ls[hf][:, cols] = alpha * ls[hf][:, cols] + jnp.sum(p, axis=0, keepdims=True)
        ms[hf][:, cols] = m_new
        return p.astype(BF16), alpha

    def pv(vc, pa, u):
        hf, cb = u
        cols = slice(cb * cw, (cb + 1) * cw)
        p, alpha = pa
        accs[hf][:, cols] = alpha * accs[hf][:, cols] + jnp.dot(vc, p, preferred_element_type=F32)

    def step(j, diag):
        kc = k_ref[pl.ds(pl.multiple_of(j * tk, tk), tk), :]
        vc = vT_ref[j]
        live = [u for u in units if diag is None or (u[1] + 1) * cw > diag * tk]
        n = len(live)
        s, pa = {}, {}
        lag = ATTN_STAGE_LAG
        for t in range(n + 2 * lag):
            if t < n:
                s[t] = qk(kc, live[t])
            if 0 <= t - lag < n:
                pa[t - lag] = softmax(s.pop(t - lag), live[t - lag], diag)
            if 0 <= t - 2 * lag < n:
                pv(vc, pa.pop(t - 2 * lag), live[t - 2 * lag])

    def body(j, c):
        step(j, None)
        return c

    kpt = tq // tk
    lax.fori_loop(0, qi * kpt, body, 0)
    for c in range(kpt):
        step(qi * kpt + c, c)

    lam = _lambda_value(lq1, lk1, lq2, lk2, lam_init)
    o = acc1[...] / l1[...] - lam * (acc2[...] / l2[...])
    on = o * lax.rsqrt(jnp.mean(o * o, axis=0, keepdims=True) + RMS_EPS) * (1.0 - lam_init)
    o_ref[...] = (on.T * g_ref[...]).astype(o_ref.dtype)


def _attn_prompt(qT, k_bf, vT, lams, subln_g, *, nheads, dk, lam_init):
    b, att, s = qT.shape
    dv = att // nheads
    tq = min(ATTN_TILE, s)
    nq = s // tq
    nk, tk = vT.shape[1] // nheads, vT.shape[3]
    lam_spec = pl.BlockSpec((1, dk), lambda bi, hi, qi: (0, 0))
    return pl.pallas_call(
        functools.partial(_attn_kernel, dk=dk, lam_init=lam_init),
        grid=(b, nheads, nq),
        in_specs=[pl.BlockSpec((None, dv, tq), lambda bi, hi, qi: (bi, hi, qi)),
                  pl.BlockSpec((None, s, dv), lambda bi, hi, qi: (bi, 0, hi)),
                  pl.BlockSpec((None, nk, dv, tk), lambda bi, hi, qi: (bi, hi, 0, 0)),
                  lam_spec, lam_spec, lam_spec, lam_spec,
                  pl.BlockSpec((1, dv), lambda bi, hi, qi: (0, 0))],
        out_specs=pl.BlockSpec((None, tq, dv), lambda bi, hi, qi: (bi, qi, hi)),
        out_shape=jax.ShapeDtypeStruct((b, s, att), BF16),
        scratch_shapes=[pltpu.VMEM((dv, tq), F32), pltpu.VMEM((dv, tq), F32),
                        pltpu.VMEM((1, tq), F32), pltpu.VMEM((1, tq), F32),
                        pltpu.VMEM((1, tq), F32), pltpu.VMEM((1, tq), F32)],
        compiler_params=_params("parallel", "parallel", "arbitrary"),
        name="attn_prompt",
    )(qT, k_bf, vT, *lams, subln_g)


def _sattn_kernel(pt_ref, *rest, pps, nheads, dk, lam_init, carry_dispatch):
    if carry_dispatch:
        dest_ref, last_ref, padded_ref, nu_ref, q_ref, kn_ref, vn_ref, h_ref = rest[:8]
        rest = rest[8:]
    else:
        q_ref, kn_ref, vn_ref = rest[:3]
        rest = rest[3:]
    k_refs = rest[:pps]
    v_refs = rest[pps:2 * pps]
    if carry_dispatch:
        lq1, lk1, lq2, lk2, g_ref, o_ref, xs_hbm, m_sc, l_sc, acc_sc, zbuf, sem = rest[2 * pps:]
    else:
        lq1, lk1, lq2, lk2, g_ref, o_ref, m_sc, l_sc, acc_sc = rest[2 * pps:]
    j = pl.program_id(1)
    if carry_dispatch:
        step = pl.program_id(0) * pl.num_programs(1) + j
        tps = h_ref.shape[0]

        @pl.when(step == 0)
        def _():
            _zero_unrouted_blocks(last_ref, padded_ref, nu_ref, xs_hbm, zbuf, sem.at[1])

        def start_rows(t0, t1):
            for t in range(t0, min(t1, tps)):
                for k in range(TOP_K):
                    d = dest_ref[(step * tps + t) * TOP_K + k]
                    pltpu.make_async_copy(h_ref.at[pl.ds(t, 1)], xs_hbm.at[pl.ds(d, 1)], sem.at[0]).start()
    else:
        def start_rows(t0, t1):
            pass
    nt = q_ref.shape[0]
    dv = 2 * dk
    prow = k_refs[0].shape[0]
    rows = 2 * nheads * nt
    nt_dims = (((1,), (1,)), ((), ()))
    head_shift = (2 * nt).bit_length() - 1

    q8 = q_ref[...]
    first = lax.broadcasted_iota(I32, (nt, dv), 1) < dk
    pieces = []
    for h in range(nheads):
        qh = q8[:, h * dv:(h + 1) * dv]
        pieces += [jnp.where(first, qh, 0.0), jnp.where(first, 0.0, qh)]
    qall = jnp.concatenate(pieces, axis=0).astype(BF16)

    @pl.when(j == 0)
    def _():
        m_sc[...] = jnp.full(m_sc.shape, -jnp.inf, F32)
        l_sc[...] = jnp.zeros(l_sc.shape, F32)
        acc_sc[...] = jnp.zeros(acc_sc.shape, F32)

    def soft(s):
        m = jnp.max(s, axis=1, keepdims=True)
        p = jnp.exp2(s - m).astype(BF16)
        return m, jnp.sum(p.astype(F32), axis=1, keepdims=True), p

    def weigh(p, vals, width):
        pv = None
        for i, v in enumerate(vals):
            d = jnp.dot(p[:, i * width:(i + 1) * width], v, preferred_element_type=F32)
            pv = d if pv is None else pv + d
        return pv

    def merge(parts):
        m = parts[0][0]
        for part in parts[1:]:
            m = jnp.maximum(m, part[0])
        l = acc = None
        for m_p, l_p, acc_p in parts:
            w = jnp.exp2(m_p - m)
            l = w * l_p if l is None else l + w * l_p
            acc = w * acc_p if acc is None else acc + w * acc_p
        return m, l, acc

    row_head = lax.broadcasted_iota(I32, (rows, prow), 0) >> head_shift
    col_head = lax.broadcasted_iota(I32, (rows, prow), 1) & (nheads - 1)
    own = row_head == col_head
    groups = [range(g0, min(g0 + SAMPLE_PAGE_GROUP, pps)) for g0 in range(0, pps, SAMPLE_PAGE_GROUP)]
    parts = [(m_sc[...], l_sc[...], acc_sc[...])]
    n, lag = len(groups), SAMPLE_STAGE_LAG
    s_of, soft_of = {}, {}
    rows_per_stage = -(-h_ref.shape[0] // (n + 2 * lag)) if carry_dispatch else 0
    for t in range(n + 2 * lag):
        start_rows(t * rows_per_stage, (t + 1) * rows_per_stage)
        if t < n:
            s_of[t] = jnp.concatenate(
                [jnp.where(own, lax.dot_general(qall, k_refs[i][...].astype(BF16), nt_dims,
                                                preferred_element_type=F32), -jnp.inf)
                 for i in groups[t]], axis=1)
        if 0 <= t - lag < n:
            soft_of[t - lag] = soft(s_of.pop(t - lag))
        if 0 <= t - 2 * lag < n:
            m_g, l_g, p_g = soft_of.pop(t - 2 * lag)
            parts.append((m_g, l_g, weigh(p_g, [v_refs[i][...].astype(BF16) for i in groups[t - 2 * lag]],
                                          prow)))
    state = merge(parts)
    m_sc[...], l_sc[...], acc_sc[...] = state

    @pl.when(j == pl.num_programs(1) - 1)
    def _():
        nnew = nheads * nt
        zpad = jnp.zeros((LANES - nnew, dv), F32)
        kn = jnp.concatenate([kn_ref[...], zpad], axis=0).astype(BF16)
        vn = jnp.concatenate([vn_ref[...], zpad], axis=0).astype(BF16)
        sn = lax.dot_general(qall, kn, nt_dims, preferred_element_type=F32)
        r = lax.broadcasted_iota(I32, sn.shape, 0)
        c = lax.broadcasted_iota(I32, sn.shape, 1)
        vis = jnp.logical_and((r >> head_shift) == (c & (nheads - 1)),
                              (c >> (nheads.bit_length() - 1)) <= (r & (nt - 1)))
        vis = jnp.logical_and(vis, c < nnew)
        m_n, l_n, p_n = soft(jnp.where(vis, sn, -jnp.inf))
        _, l_fin, acc_fin = merge([state, (m_n, l_n, weigh(p_n, [vn], LANES))])

        lam = _lambda_value(lq1, lk1, lq2, lk2, lam_init)
        o = acc_fin / l_fin
        for h in range(nheads):
            r0 = 2 * h * nt
            d = o[r0:r0 + nt, :] - lam * o[r0 + nt:r0 + 2 * nt, :]
            dn = d * lax.rsqrt(jnp.mean(d * d, axis=-1, keepdims=True) + RMS_EPS)
            o_ref[:, h * dv:(h + 1) * dv] = dn * g_ref[...] * (1.0 - lam_init)

    if carry_dispatch:
        for k in range(TOP_K):
            pltpu.make_async_copy(h_ref, xs_hbm.at[pl.ds(0, h_ref.shape[0])], sem.at[0]).wait()


def _attn_sample(q, k_new, v_new, cache_k, cache_v, page_table, lams, subln_g, *,
                 nheads, dk, dec_seq, lam_init, dispatch=None):
    t, att = q.shape
    nseq = t // dec_seq
    dv = 2 * dk
    npool, page = cache_k.shape[0], cache_k.shape[1]
    npages = page_table.shape[1]
    pps = min(PAGES_PER_STEP, npages)
    nsteps = npages // pps
    prow = page * nheads
    assert nheads & (nheads - 1) == 0 and nheads * dec_seq <= LANES
    ck = cache_k.reshape(npool, prow, dv)
    cv = cache_v.reshape(npool, prow, dv)
    tok_spec = pl.BlockSpec((dec_seq, att), lambda s, j, *_: (s, 0))
    new_spec = pl.BlockSpec((dec_seq * nheads, dv), lambda s, j, *_: (s, 0))
    lam_spec = pl.BlockSpec((1, dk), lambda s, j, *_: (0, 0))

    def page_spec(i):
        return pl.BlockSpec((None, prow, dv),
                            lambda s, j, pt, *_: (pt[s * npages + j * pps + i], 0, 0))

    rows = 2 * nheads * dec_seq
    scalars = [page_table.reshape(-1)]
    tensors = [q, k_new, v_new]
    in_specs = [tok_spec, new_spec, new_spec]
    out_specs = [tok_spec]
    out_shape = [jax.ShapeDtypeStruct((t, att), F32)]
    scratch = [pltpu.VMEM((rows, 1), F32), pltpu.VMEM((rows, 1), F32), pltpu.VMEM((rows, dv), F32)]
    if dispatch is not None:
        dest, last_blk, padded, n_used, h2, n_rows = dispatch
        th, d = h2.shape
        tps = th // (nseq * nsteps)
        assert tps * nseq * nsteps == th and tps % SUBLANES == 0
        scalars += [dest.T.reshape(-1), last_blk, padded, n_used]
        tensors.append(h2)
        in_specs.append(pl.BlockSpec((tps, d), lambda s, j, *_: (s * nsteps + j, 0)))
        out_specs.append(pl.BlockSpec(memory_space=pl.ANY))
        out_shape.append(jax.ShapeDtypeStruct((n_rows, d), h2.dtype))
        scratch += [pltpu.VMEM((EXPERT_BLOCK, d), h2.dtype), pltpu.SemaphoreType.DMA((2,))]
    grid_spec = pltpu.PrefetchScalarGridSpec(
        num_scalar_prefetch=len(scalars),
        grid=(nseq, nsteps),
        in_specs=in_specs + [page_spec(i) for i in range(pps)] + [page_spec(i) for i in range(pps)]
                 + [lam_spec] * 4 + [pl.BlockSpec((1, dv), lambda s, j, *_: (0, 0))],
        out_specs=out_specs,
        scratch_shapes=scratch,
    )
    outs = pl.pallas_call(
        functools.partial(_sattn_kernel, pps=pps, nheads=nheads, dk=dk, lam_init=lam_init,
                          carry_dispatch=dispatch is not None),
        grid_spec=grid_spec,
        out_shape=out_shape,
        compiler_params=_params("arbitrary", "arbitrary"),
        name="attn_sample",
    )(*scalars, *tensors, *([ck] * pps), *([cv] * pps), *lams, subln_g)
    return outs if dispatch is not None else outs[0]


def _post_kernel(x_ref, a_ref, yc_ref, gate_ref, shift_ref, scale_ref, woa_ref, woc_ref, g2_ref,
                 wr_ref, br_ref, tri_ref, x1_out, h2_out, ri_out, ptok_out, cnt_out, carry_sc, *, ne):
    i = pl.program_id(0)
    tm = x_ref.shape[0]

    @pl.when(i == 0)
    def _():
        carry_sc[...] = jnp.zeros(carry_sc.shape, F32)

    mix = (jnp.dot(a_ref[...].astype(BF16), woa_ref[...], preferred_element_type=F32)
           + jnp.dot(yc_ref[...], woc_ref[...], preferred_element_type=F32))
    x1 = x_ref[...] + gate_ref[...] * mix
    ms = jnp.mean(x1 * x1, axis=-1, keepdims=True)
    h2 = (x1 * lax.rsqrt(ms + RMS_EPS) * g2_ref[...]) * (1.0 + scale_ref[...]) + shift_ref[...]
    x1_out[...] = x1
    h2_out[...] = h2

    logits = _dot3(h2, wr_ref[...]) + br_ref[...]
    lt = logits.T[0:ne, :]
    eidx = lax.broadcasted_iota(I32, lt.shape, 0)
    vals, idxs = [], []
    for _ in range(TOP_K):
        mx = jnp.max(lt, axis=0, keepdims=True)
        ik = jnp.min(jnp.where(lt == mx, eidx, ne), axis=0, keepdims=True)
        vals.append(mx)
        idxs.append(ik)
        lt = jnp.where(eidx == ik, -jnp.inf, lt)
    ex = [jnp.exp(v - vals[0]) for v in vals]
    den = ex[0] + ex[1] + ex[2] + ex[3]
    probs = [e / den for e in ex]

    sel = jnp.zeros(lt.shape, F32)
    for ik in idxs:
        sel = sel + jnp.where(eidx == ik, 1.0, 0.0)
    cum = jnp.dot(sel.astype(BF16), tri_ref[...], preferred_element_type=F32) + carry_sc[...]
    ranks = [jnp.sum(jnp.where(eidx == ik, cum, 0.0), axis=0, keepdims=True) for ik in idxs]
    carry_new = carry_sc[...] + jnp.sum(sel, axis=1, keepdims=True)
    carry_sc[...] = carry_new
    cnt_out[...] = jnp.broadcast_to(carry_new, cnt_out.shape)

    ri_out[...] = jnp.concatenate(idxs + [r.astype(I32) for r in ranks], axis=0)
    ppad = jnp.concatenate(probs + [jnp.zeros((LANES - TOP_K, tm), F32)], axis=0)
    ptok_out[...] = ppad.T


def _post(x, a, yc, mod_specs, mods, woa, woc, g2, wr_pad, br_pad, *, seq_len, ne):
    t, d = x.shape
    att = a.shape[1]
    conv = yc.shape[1]
    tm = min(TOKEN_TILE, t)
    nt = t // tm
    tri = (lax.broadcasted_iota(I32, (tm, tm), 0) < lax.broadcasted_iota(I32, (tm, tm), 1)).astype(BF16)
    const = lambda shape: pl.BlockSpec(shape, lambda i: (0,) * len(shape))
    row = lambda w: pl.BlockSpec((tm, w), lambda i: (i, 0))
    return pl.pallas_call(
        functools.partial(_post_kernel, ne=ne),
        grid=(nt,),
        in_specs=[row(d), row(att), row(conv), mod_specs(2, tm), mod_specs(3, tm), mod_specs(4, tm),
                  const((att, d)), const((conv, d)), const((1, d)), const((d, LANES)), const((1, LANES)),
                  const((tm, tm))],
        out_specs=[row(d), row(d), pl.BlockSpec((2 * TOP_K, tm), lambda i: (0, i)), row(LANES),
                   const((ne, LANES))],
        out_shape=[jax.ShapeDtypeStruct((t, d), F32), jax.ShapeDtypeStruct((t, d), F32),
                   jax.ShapeDtypeStruct((2 * TOP_K, t), I32), jax.ShapeDtypeStruct((t, LANES), F32),
                   jax.ShapeDtypeStruct((ne, LANES), F32)],
        scratch_shapes=[pltpu.VMEM((ne, 1), F32)],
        compiler_params=_params("arbitrary"),
        name="post_mix",
    )(x, a, yc, mods, mods, mods, woa, woc, g2, wr_pad, br_pad, tri)


def _zero_unrouted_blocks(last_ref, padded_ref, nu_ref, xs_hbm, zbuf, sem):
    bm = zbuf.shape[0]
    ne = last_ref.shape[0]
    nblk = xs_hbm.shape[0] // bm
    zbuf[...] = jnp.zeros(zbuf.shape, zbuf.dtype)

    def zero_block(b):
        return pltpu.make_async_copy(zbuf, xs_hbm.at[pl.ds(pl.multiple_of(b * bm, bm), bm)], sem)

    def tail_start(b, c):
        zero_block(b).start()
        return c

    def tail_wait(b, c):
        zero_block(b).wait()
        return c

    for e in range(ne):
        @pl.when(padded_ref[e] > 0)
        def _():
            zero_block(last_ref[e]).start()
    lax.fori_loop(nu_ref[0], nblk, tail_start, 0)
    for e in range(ne):
        @pl.when(padded_ref[e] > 0)
        def _():
            zero_block(last_ref[e]).wait()
    lax.fori_loop(nu_ref[0], nblk, tail_wait, 0)


def _dispatch_kernel(last_ref, padded_ref, nu_ref, dest_hbm, h_ref, xs_hbm, idx_smem, zbuf, sem, *, tmd):
    i = pl.program_id(0)

    @pl.when(i == 0)
    def _():
        _zero_unrouted_blocks(last_ref, padded_ref, nu_ref, xs_hbm, zbuf, sem.at[2])

    cp = pltpu.make_async_copy(dest_hbm.at[i], idx_smem, sem.at[0])
    cp.start()
    cp.wait()

    def issue(g, c):
        for u in range(ROW_UNROLL):
            t = g * ROW_UNROLL + u
            for k in range(TOP_K):
                d = idx_smem[k * tmd + t]
                pltpu.make_async_copy(h_ref.at[pl.ds(t, 1)], xs_hbm.at[pl.ds(d, 1)], sem.at[1]).start()
        return c

    lax.fori_loop(0, tmd // ROW_UNROLL, issue, 0)
    for k in range(TOP_K):
        pltpu.make_async_copy(h_ref, xs_hbm.at[pl.ds(0, tmd)], sem.at[1]).wait()


def _dispatch(last_blk, padded, n_used, dest_tiles, h2, n_rows):
    t, d = h2.shape
    nt, width = dest_tiles.shape
    tmd = width // TOP_K
    grid_spec = pltpu.PrefetchScalarGridSpec(
        num_scalar_prefetch=3,
        grid=(nt,),
        in_specs=[pl.BlockSpec(memory_space=pl.ANY), pl.BlockSpec((tmd, d), lambda i, *_: (i, 0))],
        out_specs=pl.BlockSpec(memory_space=pl.ANY),
        scratch_shapes=[pltpu.SMEM((width,), I32), pltpu.VMEM((EXPERT_BLOCK, d), h2.dtype),
                        pltpu.SemaphoreType.DMA((3,))],
    )
    return pl.pallas_call(
        functools.partial(_dispatch_kernel, tmd=tmd),
        grid_spec=grid_spec,
        out_shape=jax.ShapeDtypeStruct((n_rows, d), h2.dtype),
        compiler_params=_params("arbitrary"),
        name="moe_dispatch",
    )(last_blk, padded, n_used, dest_tiles, h2)


def _expert_kernel(be_ref, nu_ref, x_ref, wg_ref, bg_ref, wu_ref, bu_ref, wd_ref, bd_ref, y_ref,
                   wg_sc, wu_sc, wd_sc):
    i = pl.program_id(0)
    used = i < nu_ref[0]
    prev = be_ref[jnp.maximum(i - 1, 0)]
    fresh = jnp.logical_or(i == 0, be_ref[i] != prev)

    @pl.when(jnp.logical_and(used, fresh))
    def _():
        wg_sc[...] = wg_ref[...].astype(BF16)
        wu_sc[...] = wu_ref[...].astype(BF16)
        wd_sc[...] = wd_ref[...].astype(BF16)

    @pl.when(used)
    def _():
        x = x_ref[...].astype(BF16)
        g = jnp.minimum(jnp.dot(x, wg_sc[...], preferred_element_type=F32) + bg_ref[...], SWIGLU_LIMIT)
        u = jnp.clip(jnp.dot(x, wu_sc[...], preferred_element_type=F32) + bu_ref[...],
                     -SWIGLU_LIMIT, SWIGLU_LIMIT)
        a = g * jax.nn.sigmoid(SWIGLU_ALPHA * g) * (u + 1.0)
        y_ref[...] = jnp.dot(a.astype(BF16), wd_sc[...], preferred_element_type=F32) + bd_ref[...]

    @pl.when(jnp.logical_not(used))
    def _():
        y_ref[...] = jnp.zeros(y_ref.shape, F32)


def _experts(block_e, n_used, xs, w_gate, b_gate, w_up, b_up, w_down, b_down):
    n_rows, d = xs.shape
    ne, _, dff = w_gate.shape
    bm = EXPERT_BLOCK
    nblk = n_rows // bm

    def blk(i, be, nu):
        return jnp.minimum(i, nu[0] - 1)

    row_spec = pl.BlockSpec((bm, d), lambda i, be, nu: (blk(i, be, nu), 0))
    w_spec = lambda k, n: pl.BlockSpec((None, k, n), lambda i, be, nu: (be[blk(i, be, nu)], 0, 0))
    grid_spec = pltpu.PrefetchScalarGridSpec(
        num_scalar_prefetch=2,
        grid=(nblk,),
        in_specs=[row_spec, w_spec(d, dff), w_spec(1, dff), w_spec(d, dff), w_spec(1, dff),
                  w_spec(dff, d), w_spec(1, d)],
        out_specs=pl.BlockSpec((bm, d), lambda i, be, nu: (i, 0)),
        scratch_shapes=[pltpu.VMEM((d, dff), BF16), pltpu.VMEM((d, dff), BF16), pltpu.VMEM((dff, d), BF16)],
    )
    return pl.pallas_call(
        _expert_kernel,
        grid_spec=grid_spec,
        out_shape=jax.ShapeDtypeStruct((n_rows, d), F32),
        compiler_params=_params("arbitrary"),
        name="moe_experts",
    )(block_e, n_used, xs, w_gate, b_gate.reshape(ne, 1, dff), w_up, b_up.reshape(ne, 1, dff),
      w_down, b_down.reshape(ne, 1, d))


def _combine_kernel(dest_hbm, x1_ref, gate_ref, p_ref, yb_hbm, o_ref, idx_smem, buf, sem, *, tmc):
    i = pl.program_id(0)
    cp = pltpu.make_async_copy(dest_hbm.at[i], idx_smem, sem.at[0])
    cp.start()
    cp.wait()

    def issue(g, c):
        for u in range(ROW_UNROLL):
            t = g * ROW_UNROLL + u
            for k in range(TOP_K):
                d = idx_smem[k * tmc + t]
                pltpu.make_async_copy(yb_hbm.at[pl.ds(d, 1)], buf.at[k, pl.ds(t, 1)], sem.at[1]).start()
        return c

    lax.fori_loop(0, tmc // ROW_UNROLL, issue, 0)
    for k in range(TOP_K):
        pltpu.make_async_copy(yb_hbm.at[pl.ds(0, tmc)], buf.at[k], sem.at[1]).wait()

    p = p_ref[...]
    y = p[:, 0:1] * buf[0]
    for k in range(1, TOP_K):
        y = y + p[:, k:k + 1] * buf[k]
    o_ref[...] = x1_ref[...] + gate_ref[...] * y


def _combine(dest_tiles, x1, mod_specs, mods, ptok, yb):
    t, d = x1.shape
    nt, width = dest_tiles.shape
    tmc = width // TOP_K
    row = lambda w: pl.BlockSpec((tmc, w), lambda i: (i, 0))
    return pl.pallas_call(
        functools.partial(_combine_kernel, tmc=tmc),
        grid=(nt,),
        in_specs=[pl.BlockSpec(memory_space=pl.ANY), row(d), mod_specs(5, tmc), row(LANES),
                  pl.BlockSpec(memory_space=pl.ANY)],
        out_specs=row(d),
        out_shape=jax.ShapeDtypeStruct((t, d), F32),
        scratch_shapes=[pltpu.SMEM((width,), I32), pltpu.VMEM((TOP_K, tmc, d), F32),
                        pltpu.SemaphoreType.DMA((2,))],
        compiler_params=_params("arbitrary"),
        name="moe_combine",
    )(dest_tiles, x1, mods, ptok, yb)


def _dest_tiles(dest, tile):
    t = dest.shape[1]
    return dest.reshape(TOP_K, t // tile, tile).transpose(1, 0, 2).reshape(t // tile, TOP_K * tile)


def _moe_plan(ri, cnt, t, ne):
    bm = EXPERT_BLOCK
    n_rows = -(-(t * TOP_K) // bm) * bm + ne * bm
    counts = cnt[:, 0].astype(I32)
    padded = (counts + bm - 1) // bm * bm
    pend = jnp.cumsum(padded)
    pstart = pend - padded
    eids = jnp.arange(ne, dtype=I32)
    dest = ri[TOP_K:2 * TOP_K] + jnp.sum(
        jnp.where(ri[0:TOP_K, :, None] == eids, pstart, 0), axis=-1)
    nblk = n_rows // bm
    first_row = jnp.arange(nblk, dtype=I32) * bm
    block_e = jnp.minimum(jnp.sum((pend[None, :] <= first_row[:, None]).astype(I32), axis=1), ne - 1)
    return dict(dest=dest, n_rows=n_rows, block_e=block_e, n_used=(pend[-1:] // bm).astype(I32),
                last_blk=(pend // bm - 1).astype(I32), padded=padded.astype(I32))


def _moe_dispatch(plan, h2):
    tmd = min(TOKEN_TILE, h2.shape[0])
    return _dispatch(plan["last_blk"], plan["padded"], plan["n_used"], _dest_tiles(plan["dest"], tmd),
                     h2, plan["n_rows"])


def _moe_finish(plan, xs, x1, ptok, mod_specs, mods, w_gate, b_gate, w_up, b_up, w_down, b_down):
    tmc = min(COMBINE_TILE, x1.shape[0])
    yb = _experts(plan["block_e"], plan["n_used"], xs, w_gate, b_gate, w_up, b_up, w_down, b_down)
    return _combine(_dest_tiles(plan["dest"], tmc), x1, mod_specs, mods, ptok, yb)


def kernel(x_prompt, x_sample, cache_k, cache_v, state_conv, page_table, c_prompt, c_sample,
           w_ada, b_ada, norm1_g, w_in, q_norm_g, k_norm_g, lambda_q1, lambda_k1, lambda_q2,
           lambda_k2, subln_g, conv_w, w_out, norm2_g, w_router, b_router, w_gate, b_gate,
           w_up, b_up, w_down, b_down):
    nb, seq, d = x_prompt.shape
    db, dec_seq, _ = x_sample.shape
    depth, _, page, nheads, dv = cache_k.shape
    dk = dv // 2
    att = nheads * dv
    conv = conv_w.shape[2]
    ne = w_router.shape[2]
    tp, ts = nb * seq, db * dec_seq
    assert dec_seq == SUBLANES and att % LANES == 0 and conv % LANES == 0 and ne <= LANES
    assert dk & (dk - 1) == 0

    c_rows = -(-(nb + db) // SUBLANES) * SUBLANES
    c_all = jnp.concatenate([c_prompt, c_sample, jnp.zeros((c_rows - nb - db, d), F32)], axis=0)
    gidx = jnp.arange(att, dtype=I32) // dk
    gmat = (gidx[:, None] == gidx[None, :]).astype(BF16)

    yp = x_prompt.reshape(tp, d)
    ys = x_sample.reshape(ts, d)
    outs = [[] for _ in range(6)]
    for l in range(depth):
        lam_init = 0.8 - 0.6 * math.exp(-0.3 * l)
        mod_all = _adaln(c_all, w_ada[l], b_ada[l])
        mod_p = mod_all[:nb].reshape(nb, N_ADA, 1, d)
        mod_s = jnp.repeat(mod_all[nb:nb + db], dec_seq, axis=0)

        def mod_specs_p(j, tile):
            per_batch = max(seq // tile, 1)
            return pl.BlockSpec((None, None, 1, d), lambda i: (i // per_batch, j, 0, 0))

        def mod_specs_s(j, tile):
            return pl.BlockSpec((tile, d), lambda i: (i, j))

        win_bf = w_in[l].astype(BF16)
        woa = w_out[l][:att].astype(BF16)
        woc = w_out[l][att:].astype(BF16)
        g1 = norm1_g[l].reshape(1, d)
        g2 = norm2_g[l].reshape(1, d)
        gq = jnp.tile(q_norm_g[l], att // dk).reshape(1, att)
        gk = jnp.tile(k_norm_g[l], att // dk).reshape(1, att)
        lams = [v[l].reshape(1, dk) for v in (lambda_q1, lambda_k1, lambda_q2, lambda_k2)]
        sg = subln_g[l].reshape(1, dv)
        wr_pad = jnp.pad(w_router[l], ((0, 0), (0, LANES - ne)))
        br_pad = jnp.pad(b_router[l], (0, LANES - ne)).reshape(1, LANES)
        moe_w = (w_gate[l], b_gate[l], w_up[l], b_up[l], w_down[l], b_down[l])

        q_bf, k_f, v_f, k_bf, v_bf, yc, conv_p = _premix(
            yp, mod_specs_p, mod_p, g1, win_bf, gq, gk, gmat, conv_w[l], None,
            sample=False, seq_len=seq, dk=dk)
        tk = min(ATTN_KEYS, seq)
        nk = seq // tk
        qT = q_bf.reshape(nb, seq, att).transpose(0, 2, 1)
        vT = (v_bf.reshape(nb, nk, tk, nheads, dv).transpose(0, 3, 1, 4, 2)
              .reshape(nb, nheads * nk, dv, tk))
        a_p = _attn_prompt(qT, k_bf.reshape(nb, seq, att), vT, lams, sg,
                           nheads=nheads, dk=dk, lam_init=lam_init).reshape(tp, att)
        x1, h2, ri, ptok, cnt = _post(yp, a_p, yc, mod_specs_p, mod_p, woa, woc, g2, wr_pad, br_pad,
                                      seq_len=seq, ne=ne)
        plan_p = _moe_plan(ri, cnt, tp, ne)
        outs[0].append(k_f.reshape(nb, seq, nheads, dv))
        outs[1].append(v_f.reshape(nb, seq, nheads, dv))
        outs[2].append(conv_p)

        q_s, k_s, v_s, yc_s, conv_s = _premix(
            ys, mod_specs_s, mod_s, g1, win_bf, gq, gk, gmat, conv_w[l], state_conv[l],
            sample=True, seq_len=dec_seq, dk=dk)
        npages = page_table.shape[1]
        attn_steps = db * (npages // min(PAGES_PER_STEP, npages))
        carry = tp % (attn_steps * SUBLANES) == 0 and tp * TOP_K <= MAX_SMEM_DEST
        sattn = functools.partial(_attn_sample, q_s, k_s, v_s, cache_k[l], cache_v[l], page_table, lams, sg,
                                  nheads=nheads, dk=dk, dec_seq=dec_seq, lam_init=lam_init)
        if carry:
            a_s, xs_p = sattn(dispatch=(plan_p["dest"], plan_p["last_blk"], plan_p["padded"],
                                        plan_p["n_used"], h2, plan_p["n_rows"]))
        else:
            a_s, xs_p = sattn(), _moe_dispatch(plan_p, h2)
        yp = _moe_finish(plan_p, xs_p, x1, ptok, mod_specs_p, mod_p, *moe_w)

        x1, h2, ri, ptok, cnt = _post(ys, a_s, yc_s, mod_specs_s, mod_s, woa, woc, g2, wr_pad, br_pad,
                                      seq_len=dec_seq, ne=ne)
        plan_s = _moe_plan(ri, cnt, ts, ne)
        ys = _moe_finish(plan_s, _moe_dispatch(plan_s, h2), x1, ptok, mod_specs_s, mod_s, *moe_w)
        outs[3].append(k_s.reshape(db, dec_seq, nheads, dv))
        outs[4].append(v_s.reshape(db, dec_seq, nheads, dv))
        outs[5].append(conv_s)

    return (yp.reshape(nb, seq, d), ys.reshape(db, dec_seq, d),
            jnp.stack(outs[0], 0), jnp.stack(outs[1], 0), jnp.stack(outs[2], 0),
            jnp.stack(outs[3], 0), jnp.stack(outs[4], 0), jnp.stack(outs[5], 0))
```

```python
import functools
import math

import jax
import jax.numpy as jnp
from jax import lax
from jax.experimental import pallas as pl
from jax.experimental.pallas import tpu as pltpu

F32 = jnp.float32
BF16 = jnp.bfloat16
I32 = jnp.int32

RMS_EPS = 1e-6
N_ADA = 6
TOP_K = 4
CONV_WIDTH = 3
SWIGLU_ALPHA = 1.702
SWIGLU_LIMIT = 7.0
LOG2E = 1.4426950408889634

LANES = 128
SUBLANES = 8
VMEM_LIMIT = 56 * 1024 * 1024

TOKEN_TILE = 512
ATTN_TILE = 2048
ATTN_KEYS = 512
ATTN_COLS = 256
ATTN_STAGE_LAG = 2
EXPERT_BLOCK = 512
COMBINE_TILE = 256
PAGES_PER_STEP = 32
SAMPLE_PAGE_GROUP = 4
SAMPLE_STAGE_LAG = 2
ROW_UNROLL = 8
MAX_SMEM_DEST = 128 * 1024


def _params(*sem):
    return pltpu.CompilerParams(dimension_semantics=sem, vmem_limit_bytes=VMEM_LIMIT)


def _split_bf16(a):
    hi = a.astype(BF16)
    lo = (a - hi.astype(F32)).astype(BF16)
    return hi, lo


def _dot3(a, w):
    a_hi, a_lo = _split_bf16(a)
    w_hi, w_lo = _split_bf16(w)
    d = functools.partial(jnp.dot, preferred_element_type=F32)
    return d(a_hi, w_hi) + (d(a_hi, w_lo) + d(a_lo, w_hi))


def _adaln_kernel(c_ref, w_ref, b_ref, o_ref):
    c = c_ref[...]
    s = c * jax.nn.sigmoid(c)
    o_ref[...] = _dot3(s, w_ref[...]) + b_ref[...]


def _adaln(c_all, w_ada, b_ada):
    rows, d = c_all.shape
    n = w_ada.shape[1]
    tn = d
    return pl.pallas_call(
        _adaln_kernel,
        grid=(n // tn,),
        in_specs=[pl.BlockSpec((rows, d), lambda j: (0, 0)),
                  pl.BlockSpec((d, tn), lambda j: (0, j)),
                  pl.BlockSpec((1, tn), lambda j: (0, j))],
        out_specs=pl.BlockSpec((rows, tn), lambda j: (0, j)),
        out_shape=jax.ShapeDtypeStruct((rows, n), F32),
        compiler_params=_params("parallel"),
        name="adaln",
    )(c_all, w_ada, b_ada.reshape(1, n))


def _premix_kernel(x_ref, shift_ref, scale_ref, g1_ref, win_ref, gq_ref, gk_ref, gmat_ref, cw_ref,
                   *rest, sample, tiles_per_batch, att, dk, q_scale):
    if sample:
        st_ref, q_out, kf_out, vf_out, yc_out, cs_out, ubuf = rest
    else:
        q_out, kf_out, vf_out, kb_out, vb_out, yc_out, cs_out, ubuf = rest
    tm = x_ref.shape[0]
    conv = yc_out.shape[1]

    x = x_ref[...]
    ms = jnp.mean(x * x, axis=-1, keepdims=True)
    xn = x * lax.rsqrt(ms + RMS_EPS) * g1_ref[...]
    h = (xn * (1.0 + scale_ref[...]) + shift_ref[...]).astype(BF16)
    z = jnp.dot(h, win_ref[...], preferred_element_type=F32)

    def group_norm(t, g_ref):
        ss = jnp.dot((t * t).astype(BF16), gmat_ref[...], preferred_element_type=F32)
        return t * lax.rsqrt(ss * (1.0 / dk) + RMS_EPS) * g_ref[...]

    q = group_norm(z[:, 0:att], gq_ref)
    k = group_norm(z[:, att:2 * att], gk_ref)
    v = z[:, 2 * att:3 * att]
    cx = z[:, 3 * att:3 * att + conv]
    cb = z[:, 3 * att + conv:3 * att + 2 * conv]
    cc = z[:, 3 * att + 2 * conv:3 * att + 3 * conv]

    q_out[...] = (q * q_scale).astype(q_out.dtype)
    nheads = kf_out.shape[0] // tm
    dv = att // nheads
    for hd in range(nheads):
        kf_out[pl.ds(hd, tm, stride=nheads), :] = k[:, hd * dv:(hd + 1) * dv]
        vf_out[pl.ds(hd, tm, stride=nheads), :] = v[:, hd * dv:(hd + 1) * dv]
    if not sample:
        kb_out[...] = k.astype(BF16)
        vb_out[...] = v.astype(BF16)

    u = cc * cx
    w0 = cw_ref[0:1, :]
    w1 = cw_ref[1:2, :]
    w2 = cw_ref[2:3, :]
    if sample:
        nseq = tm // SUBLANES
        u3 = u.reshape(nseq, SUBLANES, conv)
        ubuf[:, 8:16, :] = u3
        ubuf[:, 6:8, :] = st_ref[...]
        um1 = ubuf[:, 7:15, :]
        um2 = ubuf[:, 6:14, :]
        y3 = w0 * um2 + w1 * um1 + w2 * u3
        y = y3.reshape(tm, conv)
        cs_out[...] = ubuf[:, 14:16, :]
    else:
        i = pl.program_id(0)

        @pl.when(i % tiles_per_batch == 0)
        def _():
            ubuf[0:8, :] = jnp.zeros((8, conv), F32)

        ubuf[8:tm + 8, :] = u
        um1 = ubuf[7:tm + 7, :]
        um2 = ubuf[6:tm + 6, :]
        y = w0 * um2 + w1 * um1 + w2 * u
        ubuf[0:8, :] = ubuf[tm:tm + 8, :]
        cs_out[...] = u[tm - 2:tm, :]
    yc_out[...] = (cb * y).astype(BF16)


def _premix(x, mod_specs, mods, g1, win_bf, gq, gk, gmat, conv_w, state, *, sample, seq_len, dk):
    t, d = x.shape
    att = gq.shape[1]
    conv = conv_w.shape[1]
    ncol = win_bf.shape[1]
    tm = min(TOKEN_TILE, t)
    nt = t // tm
    tiles_per_batch = max(seq_len // tm, 1)
    q_scale = dk ** -0.5 * LOG2E
    const = lambda shape: pl.BlockSpec(shape, lambda i: (0,) * len(shape))
    row = lambda w: pl.BlockSpec((tm, w), lambda i: (i, 0))
    dv = 2 * dk
    nheads = att // dv
    cache_rows = pl.BlockSpec((tm * nheads, dv), lambda i: (i, 0))
    cache_shape = jax.ShapeDtypeStruct((t * nheads, dv), F32)
    in_specs = [row(d), mod_specs(0, tm), mod_specs(1, tm), const((1, d)), const((d, ncol)),
                const((1, att)), const((1, att)), const((att, att)), const((CONV_WIDTH, conv))]
    args = [x, mods, mods, g1, win_bf, gq, gk, gmat, conv_w]
    if sample:
        nseq = tm // SUBLANES
        in_specs.append(pl.BlockSpec((nseq, CONV_WIDTH - 1, conv), lambda i: (i, 0, 0)))
        args.append(state)
        out_specs = [row(att), cache_rows, cache_rows, row(conv),
                     pl.BlockSpec((nseq, CONV_WIDTH - 1, conv), lambda i: (i, 0, 0))]
        out_shape = [jax.ShapeDtypeStruct((t, att), F32), cache_shape, cache_shape,
                     jax.ShapeDtypeStruct((t, conv), BF16),
                     jax.ShapeDtypeStruct((t // SUBLANES, CONV_WIDTH - 1, conv), F32)]
        scratch = [pltpu.VMEM((nseq, 2 * SUBLANES, conv), F32)]
    else:
        nb = t // seq_len
        out_specs = [row(att), cache_rows, cache_rows, row(att), row(att), row(conv),
                     pl.BlockSpec((None, CONV_WIDTH - 1, conv), lambda i: (i // tiles_per_batch, 0, 0))]
        out_shape = [jax.ShapeDtypeStruct((t, att), BF16), cache_shape, cache_shape,
                     jax.ShapeDtypeStruct((t, att), BF16),
                     jax.ShapeDtypeStruct((t, att), BF16), jax.ShapeDtypeStruct((t, conv), BF16),
                     jax.ShapeDtypeStruct((nb, CONV_WIDTH - 1, conv), F32)]
        scratch = [pltpu.VMEM((tm + SUBLANES, conv), F32)]
    return pl.pallas_call(
        functools.partial(_premix_kernel, sample=sample, tiles_per_batch=tiles_per_batch,
                          att=att, dk=dk, q_scale=q_scale),
        grid=(nt,),
        in_specs=in_specs,
        out_specs=out_specs,
        out_shape=out_shape,
        scratch_shapes=scratch,
        compiler_params=_params("arbitrary"),
        name="premix_sample" if sample else "premix_prompt",
    )(*args)


def _lambda_value(lq1, lk1, lq2, lk2, lam_init):
    a = jnp.sum(lq1[...] * lk1[...], axis=-1, keepdims=True)
    b = jnp.sum(lq2[...] * lk2[...], axis=-1, keepdims=True)
    return jnp.exp(a) - jnp.exp(b) + lam_init


def _attn_kernel(qT_ref, k_ref, vT_ref, lq1, lk1, lq2, lk2, g_ref, o_ref,
                 acc1, acc2, m1, l1, m2, l2, *, dk, lam_init):
    tq = qT_ref.shape[1]
    tk = vT_ref.shape[2]
    qi = pl.program_id(2)
    qT = qT_ref[...].astype(F32)
    half = lax.broadcasted_iota(I32, qT.shape, 0) < dk
    rhs = (jnp.where(half, qT, 0.0).astype(BF16), jnp.where(half, 0.0, qT).astype(BF16))
    accs, ms, ls = (acc1, acc2), (m1, m2), (l1, l2)
    for hf in range(2):
        accs[hf][...] = jnp.zeros(accs[hf].shape, F32)
        ms[hf][...] = jnp.full(ms[hf].shape, -jnp.inf, F32)
        ls[hf][...] = jnp.zeros(ls[hf].shape, F32)

    cw = min(ATTN_COLS, tq)
    units = [(hf, cb) for hf in range(2) for cb in range(tq // cw)]

    def qk(kc, u):
        hf, cb = u
        return jnp.dot(kc, rhs[hf][:, cb * cw:(cb + 1) * cw], preferred_element_type=F32)

    def softmax(s, u, diag):
        hf, cb = u
        cols = slice(cb * cw, (cb + 1) * cw)
        if diag is not None and cb * cw < (diag + 1) * tk - 1:
            kpos = lax.broadcasted_iota(I32, s.shape, 0) + diag * tk
            qpos = lax.broadcasted_iota(I32, s.shape, 1) + cb * cw
            s = jnp.where(kpos <= qpos, s, -jnp.inf)
        m_prev = ms[hf][:, cols]
        m_new = jnp.maximum(m_prev, jnp.max(s, axis=0, keepdims=True))
        alpha = jnp.exp2(m_prev - m_new)
        p = jnp.exp2(s - m_new)
        ls[hf][:, cols] = alpha * ls[hf][:, cols] + jnp.sum(p, axis=0, keepdims=True)
        ms[hf][:, cols] = m_new
        return p.astype(BF16), alpha

    def pv(vc, pa, u):
        hf, cb = u
        cols = slice(cb * cw, (cb + 1) * cw)
        p, alpha = pa
        accs[hf][:, cols] = alpha * accs[hf][:, cols] + jnp.dot(vc, p, preferred_element_type=F32)

    def step(j, diag):
        kc = k_ref[pl.ds(pl.multiple_of(j * tk, tk), tk), :]
        vc = vT_ref[j]
        live = [u for u in units if diag is None or (u[1] + 1) * cw > diag * tk]
        n = len(live)
        s, pa = {}, {}
        lag = ATTN_STAGE_LAG
        for t in range(n + 2 * lag):
            if t < n:
                s[t] = qk(kc, live[t])
            if 0 <= t - lag < n:
                pa[t - lag] = softmax(s.pop(t - lag), live[t - lag], diag)
            if 0 <= t - 2 * lag < n:
                pv(vc, pa.pop(t - 2 * lag), live[t - 2 * lag])

    def body(j, c):
        step(j, None)
        return c

    kpt = tq // tk
    lax.fori_loop(0, qi * kpt, body, 0)
    for c in range(kpt):
        step(qi * kpt + c, c)

    lam = _lambda_value(lq1, lk1, lq2, lk2, lam_init)
    o = acc1[...] / l1[...] - lam * (acc2[...] / l2[...])
    on = o * lax.rsqrt(jnp.mean(o * o, axis=0, keepdims=True) + RMS_EPS) * (1.0 - lam_init)
    o_ref[...] = (on.T * g_ref[...]).astype(o_ref.dtype)


def _attn_prompt(qT, k_bf, vT, lams, subln_g, *, nheads, dk, lam_init):
    b, att, s = qT.shape
    dv = att // nheads
    tq = min(ATTN_TILE, s)
    nq = s // tq
    nk, tk = vT.shape[1] // nheads, vT.shape[3]
    lam_spec = pl.BlockSpec((1, dk), lambda bi, hi, qi: (0, 0))
    return pl.pallas_call(
        functools.partial(_attn_kernel, dk=dk, lam_init=lam_init),
        grid=(b, nheads, nq),
        in_specs=[pl.BlockSpec((None, dv, tq), lambda bi, hi, qi: (bi, hi, qi)),
                  pl.BlockSpec((None, s, dv), lambda bi, hi, qi: (bi, 0, hi)),
                  pl.BlockSpec((None, nk, dv, tk), lambda bi, hi, qi: (bi, hi, 0, 0)),
                  lam_spec, lam_spec, lam_spec, lam_spec,
                  pl.BlockSpec((1, dv), lambda bi, hi, qi: (0, 0))],
        out_specs=pl.BlockSpec((None, tq, dv), lambda bi, hi, qi: (bi, qi, hi)),
        out_shape=jax.ShapeDtypeStruct((b, s, att), BF16),
        scratch_shapes=[pltpu.VMEM((dv, tq), F32), pltpu.VMEM((dv, tq), F32),
                        pltpu.VMEM((1, tq), F32), pltpu.VMEM((1, tq), F32),
                        pltpu.VMEM((1, tq), F32), pltpu.VMEM((1, tq), F32)],
        compiler_params=_params("parallel", "parallel", "arbitrary"),
        name="attn_prompt",
    )(qT, k_bf, vT, *lams, subln_g)


def _sattn_kernel(pt_ref, *rest, pps, nheads, dk, lam_init, carry_dispatch):
    if carry_dispatch:
        dest_ref, last_ref, padded_ref, nu_ref, q_ref, kn_ref, vn_ref, h_ref = rest[:8]
        rest = rest[8:]
    else:
        q_ref, kn_ref, vn_ref = rest[:3]
        rest = rest[3:]
    k_refs = rest[:pps]
    v_refs = rest[pps:2 * pps]
    if carry_dispatch:
        lq1, lk1, lq2, lk2, g_ref, o_ref, xs_hbm, m_sc, l_sc, acc_sc, zbuf, sem = rest[2 * pps:]
    else:
        lq1, lk1, lq2, lk2, g_ref, o_ref, m_sc, l_sc, acc_sc = rest[2 * pps:]
    j = pl.program_id(1)
    if carry_dispatch:
        step = pl.program_id(0) * pl.num_programs(1) + j
        tps = h_ref.shape[0]

        @pl.when(step == 0)
        def _():
            _zero_unrouted_blocks(last_ref, padded_ref, nu_ref, xs_hbm, zbuf, sem.at[1])

        def start_rows(t0, t1):
            for t in range(t0, min(t1, tps)):
                for k in range(TOP_K):
                    d = dest_ref[(step * tps + t) * TOP_K + k]
                    pltpu.make_async_copy(h_ref.at[pl.ds(t, 1)], xs_hbm.at[pl.ds(d, 1)],
                                          sem.at[0]).start(priority=k % 2)
    else:
        def start_rows(t0, t1):
            pass
    nt = q_ref.shape[0]
    dv = 2 * dk
    prow = k_refs[0].shape[0]
    rows = 2 * nheads * nt
    nt_dims = (((1,), (1,)), ((), ()))
    head_shift = (2 * nt).bit_length() - 1

    q8 = q_ref[...]
    first = lax.broadcasted_iota(I32, (nt, dv), 1) < dk
    pieces = []
    for h in range(nheads):
        qh = q8[:, h * dv:(h + 1) * dv]
        pieces += [jnp.where(first, qh, 0.0), jnp.where(first, 0.0, qh)]
    qall = jnp.concatenate(pieces, axis=0).astype(BF16)

    @pl.when(j == 0)
    def _():
        m_sc[...] = jnp.full(m_sc.shape, -jnp.inf, F32)
        l_sc[...] = jnp.zeros(l_sc.shape, F32)
        acc_sc[...] = jnp.zeros(acc_sc.shape, F32)

    def soft(s):
        m = jnp.max(s, axis=1, keepdims=True)
        p = jnp.exp2(s - m).astype(BF16)
        return m, jnp.sum(p.astype(F32), axis=1, keepdims=True), p

    def weigh(p, vals, width):
        pv = None
        for i, v in enumerate(vals):
            d = jnp.dot(p[:, i * width:(i + 1) * width], v, preferred_element_type=F32)
            pv = d if pv is None else pv + d
        return pv

    def merge(parts):
        m = parts[0][0]
        for part in parts[1:]:
            m = jnp.maximum(m, part[0])
        l = acc = None
        for m_p, l_p, acc_p in parts:
            w = jnp.exp2(m_p - m)
            l = w * l_p if l is None else l + w * l_p
            acc = w * acc_p if acc is None else acc + w * acc_p
        return m, l, acc

    row_head = lax.broadcasted_iota(I32, (rows, prow), 0) >> head_shift
    col_head = lax.broadcasted_iota(I32, (rows, prow), 1) & (nheads - 1)
    own = row_head == col_head
    groups = [range(g0, min(g0 + SAMPLE_PAGE_GROUP, pps)) for g0 in range(0, pps, SAMPLE_PAGE_GROUP)]
    parts = [(m_sc[...], l_sc[...], acc_sc[...])]
    n, lag = len(groups), SAMPLE_STAGE_LAG
    s_of, soft_of = {}, {}
    rows_per_stage = -(-h_ref.shape[0] // (n + 2 * lag)) if carry_dispatch else 0
    for t in range(n + 2 * lag):
        start_rows(t * rows_per_stage, (t + 1) * rows_per_stage)
        if t < n:
            s_of[t] = jnp.concatenate(
                [jnp.where(own, lax.dot_general(qall, k_refs[i][...].astype(BF16), nt_dims,
                                                preferred_element_type=F32), -jnp.inf)
                 for i in groups[t]], axis=1)
        if 0 <= t - lag < n:
            soft_of[t - lag] = soft(s_of.pop(t - lag))
        if 0 <= t - 2 * lag < n:
            m_g, l_g, p_g = soft_of.pop(t - 2 * lag)
            parts.append((m_g, l_g, weigh(p_g, [v_refs[i][...].astype(BF16) for i in groups[t - 2 * lag]],
                                          prow)))
    state = merge(parts)
    m_sc[...], l_sc[...], acc_sc[...] = state

    @pl.when(j == pl.num_programs(1) - 1)
    def _():
        nnew = nheads * nt
        zpad = jnp.zeros((LANES - nnew, dv), F32)
        kn = jnp.concatenate([kn_ref[...], zpad], axis=0).astype(BF16)
        vn = jnp.concatenate([vn_ref[...], zpad], axis=0).astype(BF16)
        sn = lax.dot_general(qall, kn, nt_dims, preferred_element_type=F32)
        r = lax.broadcasted_iota(I32, sn.shape, 0)
        c = lax.broadcasted_iota(I32, sn.shape, 1)
        vis = jnp.logical_and((r >> head_shift) == (c & (nheads - 1)),
                              (c >> (nheads.bit_length() - 1)) <= (r & (nt - 1)))
        vis = jnp.logical_and(vis, c < nnew)
        m_n, l_n, p_n = soft(jnp.where(vis, sn, -jnp.inf))
        _, l_fin, acc_fin = merge([state, (m_n, l_n, weigh(p_n, [vn], LANES))])

        lam = _lambda_value(lq1, lk1, lq2, lk2, lam_init)
        o = acc_fin / l_fin
        for h in range(nheads):
            r0 = 2 * h * nt
            d = o[r0:r0 + nt, :] - lam * o[r0 + nt:r0 + 2 * nt, :]
            dn = d * lax.rsqrt(jnp.mean(d * d, axis=-1, keepdims=True) + RMS_EPS)
            o_ref[:, h * dv:(h + 1) * dv] = dn * g_ref[...] * (1.0 - lam_init)

    if carry_dispatch:
        for k in range(TOP_K):
            pltpu.make_async_copy(h_ref, xs_hbm.at[pl.ds(0, h_ref.shape[0])], sem.at[0]).wait()


def _attn_sample(q, k_new, v_new, cache_k, cache_v, page_table, lams, subln_g, *,
                 nheads, dk, dec_seq, lam_init, dispatch=None):
    t, att = q.shape
    nseq = t // dec_seq
    dv = 2 * dk
    npool, page = cache_k.shape[0], cache_k.shape[1]
    npages = page_table.shape[1]
    pps = min(PAGES_PER_STEP, npages)
    nsteps = npages // pps
    prow = page * nheads
    assert nheads & (nheads - 1) == 0 and nheads * dec_seq <= LANES
    ck = cache_k.reshape(npool, prow, dv)
    cv = cache_v.reshape(npool, prow, dv)
    tok_spec = pl.BlockSpec((dec_seq, att), lambda s, j, *_: (s, 0))
    new_spec = pl.BlockSpec((dec_seq * nheads, dv), lambda s, j, *_: (s, 0))
    lam_spec = pl.BlockSpec((1, dk), lambda s, j, *_: (0, 0))

    def page_spec(i):
        return pl.BlockSpec((None, prow, dv),
                            lambda s, j, pt, *_: (pt[s * npages + j * pps + i], 0, 0))

    rows = 2 * nheads * dec_seq
    scalars = [page_table.reshape(-1)]
    tensors = [q, k_new, v_new]
    in_specs = [tok_spec, new_spec, new_spec]
    out_specs = [tok_spec]
    out_shape = [jax.ShapeDtypeStruct((t, att), F32)]
    scratch = [pltpu.VMEM((rows, 1), F32), pltpu.VMEM((rows, 1), F32), pltpu.VMEM((rows, dv), F32)]
    if dispatch is not None:
        dest, last_blk, padded, n_used, h2, n_rows = dispatch
        th, d = h2.shape
        tps = th // (nseq * nsteps)
        assert tps * nseq * nsteps == th and tps % SUBLANES == 0
        scalars += [dest.T.reshape(-1), last_blk, padded, n_used]
        tensors.append(h2)
        in_specs.append(pl.BlockSpec((tps, d), lambda s, j, *_: (s * nsteps + j, 0)))
        out_specs.append(pl.BlockSpec(memory_space=pl.ANY))
        out_shape.append(jax.ShapeDtypeStruct((n_rows, d), h2.dtype))
        scratch += [pltpu.VMEM((EXPERT_BLOCK, d), h2.dtype), pltpu.SemaphoreType.DMA((2,))]
    grid_spec = pltpu.PrefetchScalarGridSpec(
        num_scalar_prefetch=len(scalars),
        grid=(nseq, nsteps),
        in_specs=in_specs + [page_spec(i) for i in range(pps)] + [page_spec(i) for i in range(pps)]
                 + [lam_spec] * 4 + [pl.BlockSpec((1, dv), lambda s, j, *_: (0, 0))],
        out_specs=out_specs,
        scratch_shapes=scratch,
    )
    outs = pl.pallas_call(
        functools.partial(_sattn_kernel, pps=pps, nheads=nheads, dk=dk, lam_init=lam_init,
                          carry_dispatch=dispatch is not None),
        grid_spec=grid_spec,
        out_shape=out_shape,
        compiler_params=_params("arbitrary", "arbitrary"),
        name="attn_sample",
    )(*scalars, *tensors, *([ck] * pps), *([cv] * pps), *lams, subln_g)
    return outs if dispatch is not None else outs[0]


def _post_kernel(x_ref, a_ref, yc_ref, gate_ref, shift_ref, scale_ref, woa_ref, woc_ref, g2_ref,
                 wr_ref, br_ref, tri_ref, x1_out, h2_out, ri_out, ptok_out, cnt_out, carry_sc, *, ne):
    i = pl.program_id(0)
    tm = x_ref.shape[0]

    @pl.when(i == 0)
    def _():
        carry_sc[...] = jnp.zeros(carry_sc.shape, F32)

    mix = (jnp.dot(a_ref[...].astype(BF16), woa_ref[...], preferred_element_type=F32)
           + jnp.dot(yc_ref[...], woc_ref[...], preferred_element_type=F32))
    x1 = x_ref[...] + gate_ref[...] * mix
    ms = jnp.mean(x1 * x1, axis=-1, keepdims=True)
    h2 = (x1 * lax.rsqrt(ms + RMS_EPS) * g2_ref[...]) * (1.0 + scale_ref[...]) + shift_ref[...]
    x1_out[...] = x1
    h2_out[...] = h2

    logits = _dot3(h2, wr_ref[...]) + br_ref[...]
    lt = logits.T[0:ne, :]
    eidx = lax.broadcasted_iota(I32, lt.shape, 0)
    vals, idxs = [], []
    for _ in range(TOP_K):
        mx = jnp.max(lt, axis=0, keepdims=True)
        ik = jnp.min(jnp.where(lt == mx, eidx, ne), axis=0, keepdims=True)
        vals.append(mx)
        idxs.append(ik)
        lt = jnp.where(eidx == ik, -jnp.inf, lt)
    ex = [jnp.exp(v - vals[0]) for v in vals]
    den = ex[0] + ex[1] + ex[2] + ex[3]
    probs = [e / den for e in ex]

    sel = jnp.zeros(lt.shape, F32)
    for ik in idxs:
        sel = sel + jnp.where(eidx == ik, 1.0, 0.0)
    cum = jnp.dot(sel.astype(BF16), tri_ref[...], preferred_element_type=F32) + carry_sc[...]
    ranks = [jnp.sum(jnp.where(eidx == ik, cum, 0.0), axis=0, keepdims=True) for ik in idxs]
    carry_new = carry_sc[...] + jnp.sum(sel, axis=1, keepdims=True)
    carry_sc[...] = carry_new
    cnt_out[...] = jnp.broadcast_to(carry_new, cnt_out.shape)

    ri_out[...] = jnp.concatenate(idxs + [r.astype(I32) for r in ranks], axis=0)
    ppad = jnp.concatenate(probs + [jnp.zeros((LANES - TOP_K, tm), F32)], axis=0)
    ptok_out[...] = ppad.T


def _post(x, a, yc, mod_specs, mods, woa, woc, g2, wr_pad, br_pad, *, seq_len, ne):
    t, d = x.shape
    att = a.shape[1]
    conv = yc.shape[1]
    tm = min(TOKEN_TILE, t)
    nt = t // tm
    tri = (lax.broadcasted_iota(I32, (tm, tm), 0) < lax.broadcasted_iota(I32, (tm, tm), 1)).astype(BF16)
    const = lambda shape: pl.BlockSpec(shape, lambda i: (0,) * len(shape))
    row = lambda w: pl.BlockSpec((tm, w), lambda i: (i, 0))
    return pl.pallas_call(
        functools.partial(_post_kernel, ne=ne),
        grid=(nt,),
        in_specs=[row(d), row(att), row(conv), mod_specs(2, tm), mod_specs(3, tm), mod_specs(4, tm),
                  const((att, d)), const((conv, d)), const((1, d)), const((d, LANES)), const((1, LANES)),
                  const((tm, tm))],
        out_specs=[row(d), row(d), pl.BlockSpec((2 * TOP_K, tm), lambda i: (0, i)), row(LANES),
                   const((ne, LANES))],
        out_shape=[jax.ShapeDtypeStruct((t, d), F32), jax.ShapeDtypeStruct((t, d), F32),
                   jax.ShapeDtypeStruct((2 * TOP_K, t), I32), jax.ShapeDtypeStruct((t, LANES), F32),
                   jax.ShapeDtypeStruct((ne, LANES), F32)],
        scratch_shapes=[pltpu.VMEM((ne, 1), F32)],
        compiler_params=_params("arbitrary"),
        name="post_mix",
    )(x, a, yc, mods, mods, mods, woa, woc, g2, wr_pad, br_pad, tri)


def _zero_unrouted_blocks(last_ref, padded_ref, nu_ref, xs_hbm, zbuf, sem):
    bm = zbuf.shape[0]
    ne = last_ref.shape[0]
    nblk = xs_hbm.shape[0] // bm
    zbuf[...] = jnp.zeros(zbuf.shape, zbuf.dtype)

    def zero_block(b):
        return pltpu.make_async_copy(zbuf, xs_hbm.at[pl.ds(pl.multiple_of(b * bm, bm), bm)], sem)

    def tail_start(b, c):
        zero_block(b).start()
        return c

    def tail_wait(b, c):
        zero_block(b).wait()
        return c

    for e in range(ne):
        @pl.when(padded_ref[e] > 0)
        def _():
            zero_block(last_ref[e]).start()
    lax.fori_loop(nu_ref[0], nblk, tail_start, 0)
    for e in range(ne):
        @pl.when(padded_ref[e] > 0)
        def _():
            zero_block(last_ref[e]).wait()
    lax.fori_loop(nu_ref[0], nblk, tail_wait, 0)


def _dispatch_kernel(last_ref, padded_ref, nu_ref, dest_hbm, h_ref, xs_hbm, idx_smem, zbuf, sem, *, tmd):
    i = pl.program_id(0)

    @pl.when(i == 0)
    def _():
        _zero_unrouted_blocks(last_ref, padded_ref, nu_ref, xs_hbm, zbuf, sem.at[2])

    cp = pltpu.make_async_copy(dest_hbm.at[i], idx_smem, sem.at[0])
    cp.start()
    cp.wait()

    def issue(g, c):
        for u in range(ROW_UNROLL):
            t = g * ROW_UNROLL + u
            for k in range(TOP_K):
                d = idx_smem[k * tmd + t]
                pltpu.make_async_copy(h_ref.at[pl.ds(t, 1)], xs_hbm.at[pl.ds(d, 1)],
                                      sem.at[1]).start(priority=k % 2)
        return c

    lax.fori_loop(0, tmd // ROW_UNROLL, issue, 0)
    for k in range(TOP_K):
        pltpu.make_async_copy(h_ref, xs_hbm.at[pl.ds(0, tmd)], sem.at[1]).wait()


def _dispatch(last_blk, padded, n_used, dest_tiles, h2, n_rows):
    t, d = h2.shape
    nt, width = dest_tiles.shape
    tmd = width // TOP_K
    grid_spec = pltpu.PrefetchScalarGridSpec(
        num_scalar_prefetch=3,
        grid=(nt,),
        in_specs=[pl.BlockSpec(memory_space=pl.ANY), pl.BlockSpec((tmd, d), lambda i, *_: (i, 0))],
        out_specs=pl.BlockSpec(memory_space=pl.ANY),
        scratch_shapes=[pltpu.SMEM((width,), I32), pltpu.VMEM((EXPERT_BLOCK, d), h2.dtype),
                        pltpu.SemaphoreType.DMA((3,))],
    )
    return pl.pallas_call(
        functools.partial(_dispatch_kernel, tmd=tmd),
        grid_spec=grid_spec,
        out_shape=jax.ShapeDtypeStruct((n_rows, d), h2.dtype),
        compiler_params=_params("arbitrary"),
        name="moe_dispatch",
    )(last_blk, padded, n_used, dest_tiles, h2)


def _expert_kernel(be_ref, nu_ref, x_ref, wg_ref, bg_ref, wu_ref, bu_ref, wd_ref, bd_ref, y_ref,
                   wg_sc, wu_sc, wd_sc):
    i = pl.program_id(0)
    used = i < nu_ref[0]
    prev = be_ref[jnp.maximum(i - 1, 0)]
    fresh = jnp.logical_or(i == 0, be_ref[i] != prev)

    @pl.when(jnp.logical_and(used, fresh))
    def _():
        wg_sc[...] = wg_ref[...].astype(BF16)
        wu_sc[...] = wu_ref[...].astype(BF16)
        wd_sc[...] = wd_ref[...].astype(BF16)

    @pl.when(used)
    def _():
        x = x_ref[...].astype(BF16)
        g = jnp.minimum(jnp.dot(x, wg_sc[...], preferred_element_type=F32) + bg_ref[...], SWIGLU_LIMIT)
        u = jnp.clip(jnp.dot(x, wu_sc[...], preferred_element_type=F32) + bu_ref[...],
                     -SWIGLU_LIMIT, SWIGLU_LIMIT)
        a = g * jax.nn.sigmoid(SWIGLU_ALPHA * g) * (u + 1.0)
        y_ref[...] = jnp.dot(a.astype(BF16), wd_sc[...], preferred_element_type=F32) + bd_ref[...]

    @pl.when(jnp.logical_not(used))
    def _():
        y_ref[...] = jnp.zeros(y_ref.shape, F32)


def _experts(block_e, n_used, xs, w_gate, b_gate, w_up, b_up, w_down, b_down):
    n_rows, d = xs.shape
    ne, _, dff = w_gate.shape
    bm = EXPERT_BLOCK
    nblk = n_rows // bm

    def blk(i, be, nu):
        return jnp.minimum(i, nu[0] - 1)

    row_spec = pl.BlockSpec((bm, d), lambda i, be, nu: (blk(i, be, nu), 0))
    w_spec = lambda k, n: pl.BlockSpec((None, k, n), lambda i, be, nu: (be[blk(i, be, nu)], 0, 0))
    grid_spec = pltpu.PrefetchScalarGridSpec(
        num_scalar_prefetch=2,
        grid=(nblk,),
        in_specs=[row_spec, w_spec(d, dff), w_spec(1, dff), w_spec(d, dff), w_spec(1, dff),
                  w_spec(dff, d), w_spec(1, d)],
        out_specs=pl.BlockSpec((bm, d), lambda i, be, nu: (i, 0)),
        scratch_shapes=[pltpu.VMEM((d, dff), BF16), pltpu.VMEM((d, dff), BF16), pltpu.VMEM((dff, d), BF16)],
    )
    return pl.pallas_call(
        _expert_kernel,
        grid_spec=grid_spec,
        out_shape=jax.ShapeDtypeStruct((n_rows, d), F32),
        compiler_params=_params("arbitrary"),
        name="moe_experts",
    )(block_e, n_used, xs, w_gate, b_gate.reshape(ne, 1, dff), w_up, b_up.reshape(ne, 1, dff),
      w_down, b_down.reshape(ne, 1, d))


def _combine_kernel(dest_hbm, x1_ref, gate_ref, p_ref, yb_hbm, o_ref, idx_smem, buf, sem, *, tmc):
    i = pl.program_id(0)
    cp = pltpu.make_async_copy(dest_hbm.at[i], idx_smem, sem.at[0])
    cp.start()
    cp.wait()

    def issue(g, c):
        for u in range(ROW_UNROLL):
            t = g * ROW_UNROLL + u
            for k in range(TOP_K):
                d = idx_smem[k * tmc + t]
                pltpu.make_async_copy(yb_hbm.at[pl.ds(d, 1)], buf.at[k, pl.ds(t, 1)],
                                      sem.at[1]).start(priority=k % 2)
        return c

    lax.fori_loop(0, tmc // ROW_UNROLL, issue, 0)
    for k in range(TOP_K):
        pltpu.make_async_copy(yb_hbm.at[pl.ds(0, tmc)], buf.at[k], sem.at[1]).wait()

    p = p_ref[...]
    y = p[:, 0:1] * buf[0]
    for k in range(1, TOP_K):
        y = y + p[:, k:k + 1] * buf[k]
    o_ref[...] = x1_ref[...] + gate_ref[...] * y


def _combine(dest_tiles, x1, mod_specs, mods, ptok, yb):
    t, d = x1.shape
    nt, width = dest_tiles.shape
    tmc = width // TOP_K
    row = lambda w: pl.BlockSpec((tmc, w), lambda i: (i, 0))
    return pl.pallas_call(
        functools.partial(_combine_kernel, tmc=tmc),
        grid=(nt,),
        in_specs=[pl.BlockSpec(memory_space=pl.ANY), row(d), mod_specs(5, tmc), row(LANES),
                  pl.BlockSpec(memory_space=pl.ANY)],
        out_specs=row(d),
        out_shape=jax.ShapeDtypeStruct((t, d), F32),
        scratch_shapes=[pltpu.SMEM((width,), I32), pltpu.VMEM((TOP_K, tmc, d), F32),
                        pltpu.SemaphoreType.DMA((2,))],
        compiler_params=_params("arbitrary"),
        name="moe_combine",
    )(dest_tiles, x1, mods, ptok, yb)


def _dest_tiles(dest, tile):
    t = dest.shape[1]
    return dest.reshape(TOP_K, t // tile, tile).transpose(1, 0, 2).reshape(t // tile, TOP_K * tile)


def _moe_plan(ri, cnt, t, ne):
    bm = EXPERT_BLOCK
    n_rows = -(-(t * TOP_K) // bm) * bm + ne * bm
    counts = cnt[:, 0].astype(I32)
    padded = (counts + bm - 1) // bm * bm
    pend = jnp.cumsum(padded)
    pstart = pend - padded
    eids = jnp.arange(ne, dtype=I32)
    dest = ri[TOP_K:2 * TOP_K] + jnp.sum(
        jnp.where(ri[0:TOP_K, :, None] == eids, pstart, 0), axis=-1)
    nblk = n_rows // bm
    first_row = jnp.arange(nblk, dtype=I32) * bm
    block_e = jnp.minimum(jnp.sum((pend[None, :] <= first_row[:, None]).astype(I32), axis=1), ne - 1)
    return dict(dest=dest, n_rows=n_rows, block_e=block_e, n_used=(pend[-1:] // bm).astype(I32),
                last_blk=(pend // bm - 1).astype(I32), padded=padded.astype(I32))


def _moe_dispatch(plan, h2):
    tmd = min(TOKEN_TILE, h2.shape[0])
    return _dispatch(plan["last_blk"], plan["padded"], plan["n_used"], _dest_tiles(plan["dest"], tmd),
                     h2, plan["n_rows"])


def _moe_finish(plan, xs, x1, ptok, mod_specs, mods, w_gate, b_gate, w_up, b_up, w_down, b_down):
    tmc = min(COMBINE_TILE, x1.shape[0])
    yb = _experts(plan["block_e"], plan["n_used"], xs, w_gate, b_gate, w_up, b_up, w_down, b_down)
    return _combine(_dest_tiles(plan["dest"], tmc), x1, mod_specs, mods, ptok, yb)


def kernel(x_prompt, x_sample, cache_k, cache_v, state_conv, page_table, c_prompt, c_sample,
           w_ada, b_ada, norm1_g, w_in, q_norm_g, k_norm_g, lambda_q1, lambda_k1, lambda_q2,
           lambda_k2, subln_g, conv_w, w_out, norm2_g, w_router, b_router, w_gate, b_gate,
           w_up, b_up, w_down, b_down):
    nb, seq, d = x_prompt.shape
    db, dec_seq, _ = x_sample.shape
    depth, _, page, nheads, dv = cache_k.shape
    dk = dv // 2
    att = nheads * dv
    conv = conv_w.shape[2]
    ne = w_router.shape[2]
    tp, ts = nb * seq, db * dec_seq
    assert dec_seq == SUBLANES and att % LANES == 0 and conv % LANES == 0 and ne <= LANES
    assert dk & (dk - 1) == 0

    c_rows = -(-(nb + db) // SUBLANES) * SUBLANES
    c_all = jnp.concatenate([c_prompt, c_sample, jnp.zeros((c_rows - nb - db, d), F32)], axis=0)
    gidx = jnp.arange(att, dtype=I32) // dk
    gmat = (gidx[:, None] == gidx[None, :]).astype(BF16)

    yp = x_prompt.reshape(tp, d)
    ys = x_sample.reshape(ts, d)
    outs = [[] for _ in range(6)]
    for l in range(depth):
        lam_init = 0.8 - 0.6 * math.exp(-0.3 * l)
        mod_all = _adaln(c_all, w_ada[l], b_ada[l])
        mod_p = mod_all[:nb].reshape(nb, N_ADA, 1, d)
        mod_s = jnp.repeat(mod_all[nb:nb + db], dec_seq, axis=0)

        def mod_specs_p(j, tile):
            per_batch = max(seq // tile, 1)
            return pl.BlockSpec((None, None, 1, d), lambda i: (i // per_batch, j, 0, 0))

        def mod_specs_s(j, tile):
            return pl.BlockSpec((tile, d), lambda i: (i, j))

        win_bf = w_in[l].astype(BF16)
        woa = w_out[l][:att].astype(BF16)
        woc = w_out[l][att:].astype(BF16)
        g1 = norm1_g[l].reshape(1, d)
        g2 = norm2_g[l].reshape(1, d)
        gq = jnp.tile(q_norm_g[l], att // dk).reshape(1, att)
        gk = jnp.tile(k_norm_g[l], att // dk).reshape(1, att)
        lams = [v[l].reshape(1, dk) for v in (lambda_q1, lambda_k1, lambda_q2, lambda_k2)]
        sg = subln_g[l].reshape(1, dv)
        wr_pad = jnp.pad(w_router[l], ((0, 0), (0, LANES - ne)))
        br_pad = jnp.pad(b_router[l], (0, LANES - ne)).reshape(1, LANES)
        moe_w = (w_gate[l], b_gate[l], w_up[l], b_up[l], w_down[l], b_down[l])

        q_bf, k_f, v_f, k_bf, v_bf, yc, conv_p = _premix(
            yp, mod_specs_p, mod_p, g1, win_bf, gq, gk, gmat, conv_w[l], None,
            sample=False, seq_len=seq, dk=dk)
        tk = min(ATTN_KEYS, seq)
        nk = seq // tk
        qT = q_bf.reshape(nb, seq, att).transpose(0, 2, 1)
        vT = (v_bf.reshape(nb, nk, tk, nheads, dv).transpose(0, 3, 1, 4, 2)
              .reshape(nb, nheads * nk, dv, tk))
        a_p = _attn_prompt(qT, k_bf.reshape(nb, seq, att), vT, lams, sg,
                           nheads=nheads, dk=dk, lam_init=lam_init).reshape(tp, att)
        x1, h2, ri, ptok, cnt = _post(yp, a_p, yc, mod_specs_p, mod_p, woa, woc, g2, wr_pad, br_pad,
                                      seq_len=seq, ne=ne)
        plan_p = _moe_plan(ri, cnt, tp, ne)
        outs[0].append(k_f.reshape(nb, seq, nheads, dv))
        outs[1].append(v_f.reshape(nb, seq, nheads, dv))
        outs[2].append(conv_p)

        q_s, k_s, v_s, yc_s, conv_s = _premix(
            ys, mod_specs_s, mod_s, g1, win_bf, gq, gk, gmat, conv_w[l], state_conv[l],
            sample=True, seq_len=dec_seq, dk=dk)
        npages = page_table.shape[1]
        attn_steps = db * (npages // min(PAGES_PER_STEP, npages))
        carry = tp % (attn_steps * SUBLANES) == 0 and tp * TOP_K <= MAX_SMEM_DEST
        sattn = functools.partial(_attn_sample, q_s, k_s, v_s, cache_k[l], cache_v[l], page_table, lams, sg,
                                  nheads=nheads, dk=dk, dec_seq=dec_seq, lam_init=lam_init)
        if carry:
            a_s, xs_p = sattn(dispatch=(plan_p["dest"], plan_p["last_blk"], plan_p["padded"],
                                        plan_p["n_used"], h2, plan_p["n_rows"]))
        else:
            a_s, xs_p = sattn(), _moe_dispatch(plan_p, h2)
        yp = _moe_finish(plan_p, xs_p, x1, ptok, mod_specs_p, mod_p, *moe_w)

        x1, h2, ri, ptok, cnt = _post(ys, a_s, yc_s, mod_specs_s, mod_s, woa, woc, g2, wr_pad, br_pad,
                                      seq_len=dec_seq, ne=ne)
        plan_s = _moe_plan(ri, cnt, ts, ne)
        ys = _moe_finish(plan_s, _moe_dispatch(plan_s, h2), x1, ptok, mod_specs_s, mod_s, *moe_w)
        outs[3].append(k_s.reshape(db, dec_seq, nheads, dv))
        outs[4].append(v_s.reshape(db, dec_seq, nheads, dv))
        outs[5].append(conv_s)

    return (yp.reshape(nb, seq, d), ys.reshape(db, dec_seq, d),
            jnp.stack(outs[0], 0), jnp.stack(outs[1], 0), jnp.stack(outs[2], 0),
            jnp.stack(outs[3], 0), jnp.stack(outs[4], 0), jnp.stack(outs[5], 0))
```

```python
import functools
import math

import jax
import jax.numpy as jnp
from jax import lax
from jax.experimental import pallas as pl
from jax.experimental.pallas import tpu as pltpu

F32 = jnp.float32
BF16 = jnp.bfloat16
I32 = jnp.int32

RMS_EPS = 1e-6
N_ADA = 6
TOP_K = 4
CONV_WIDTH = 3
SWIGLU_ALPHA = 1.702
SWIGLU_LIMIT = 7.0
LOG2E = 1.4426950408889634

LANES = 128
SUBLANES = 8
VMEM_LIMIT = 56 * 1024 * 1024

TOKEN_TILE = 512
ATTN_TILE = 4096
ATTN_KEYS = 512
ATTN_COLS = 256
ATTN_STAGE_LAG = 2
EXPERT_BLOCK = 512
COMBINE_TILE = 256
PAGES_PER_STEP = 32
SAMPLE_PAGE_GROUP = 4
SAMPLE_STAGE_LAG = 2
ROW_UNROLL = 8
MAX_SMEM_DEST = 128 * 1024


def _params(*sem):
    return pltpu.CompilerParams(dimension_semantics=sem, vmem_limit_bytes=VMEM_LIMIT)


def _split_bf16(a):
    hi = a.astype(BF16)
    lo = (a - hi.astype(F32)).astype(BF16)
    return hi, lo


def _dot3(a, w):
    a_hi, a_lo = _split_bf16(a)
    w_hi, w_lo = _split_bf16(w)
    d = functools.partial(jnp.dot, preferred_element_type=F32)
    return d(a_hi, w_hi) + (d(a_hi, w_lo) + d(a_lo, w_hi))


def _adaln_kernel(c_ref, w_ref, b_ref, o_ref):
    c = c_ref[...]
    s = c * jax.nn.sigmoid(c)
    o_ref[...] = _dot3(s, w_ref[...]) + b_ref[...]


def _adaln(c_all, w_ada, b_ada):
    rows, d = c_all.shape
    n = w_ada.shape[1]
    tn = d
    return pl.pallas_call(
        _adaln_kernel,
        grid=(n // tn,),
        in_specs=[pl.BlockSpec((rows, d), lambda j: (0, 0)),
                  pl.BlockSpec((d, tn), lambda j: (0, j)),
                  pl.BlockSpec((1, tn), lambda j: (0, j))],
        out_specs=pl.BlockSpec((rows, tn), lambda j: (0, j)),
        out_shape=jax.ShapeDtypeStruct((rows, n), F32),
        compiler_params=_params("parallel"),
        name="adaln",
    )(c_all, w_ada, b_ada.reshape(1, n))


def _premix_kernel(x_ref, shift_ref, scale_ref, g1_ref, win_ref, gq_ref, gk_ref, gmat_ref, cw_ref,
                   *rest, sample, tiles_per_batch, att, dk, q_scale):
    if sample:
        st_ref, q_out, kf_out, vf_out, yc_out, cs_out, ubuf = rest
    else:
        q_out, kf_out, vf_out, kb_out, vb_out, yc_out, cs_out, ubuf = rest
    tm = x_ref.shape[0]
    conv = yc_out.shape[1]

    x = x_ref[...]
    ms = jnp.mean(x * x, axis=-1, keepdims=True)
    xn = x * lax.rsqrt(ms + RMS_EPS) * g1_ref[...]
    h = (xn * (1.0 + scale_ref[...]) + shift_ref[...]).astype(BF16)
    z = jnp.dot(h, win_ref[...], preferred_element_type=F32)

    def group_norm(t, g_ref):
        ss = jnp.dot((t * t).astype(BF16), gmat_ref[...], preferred_element_type=F32)
        return t * lax.rsqrt(ss * (1.0 / dk) + RMS_EPS) * g_ref[...]

    q = group_norm(z[:, 0:att], gq_ref)
    k = group_norm(z[:, att:2 * att], gk_ref)
    v = z[:, 2 * att:3 * att]
    cx = z[:, 3 * att:3 * att + conv]
    cb = z[:, 3 * att + conv:3 * att + 2 * conv]
    cc = z[:, 3 * att + 2 * conv:3 * att + 3 * conv]

    q_out[...] = (q * q_scale).astype(q_out.dtype)
    nheads = kf_out.shape[0] // tm
    dv = att // nheads
    for hd in range(nheads):
        kf_out[pl.ds(hd, tm, stride=nheads), :] = k[:, hd * dv:(hd + 1) * dv]
        vf_out[pl.ds(hd, tm, stride=nheads), :] = v[:, hd * dv:(hd + 1) * dv]
    if not sample:
        kb_out[...] = k.astype(BF16)
        vb_out[...] = v.astype(BF16)

    u = cc * cx
    w0 = cw_ref[0:1, :]
    w1 = cw_ref[1:2, :]
    w2 = cw_ref[2:3, :]
    if sample:
        nseq = tm // SUBLANES
        u3 = u.reshape(nseq, SUBLANES, conv)
        ubuf[:, 8:16, :] = u3
        ubuf[:, 6:8, :] = st_ref[...]
        um1 = ubuf[:, 7:15, :]
        um2 = ubuf[:, 6:14, :]
        y3 = w0 * um2 + w1 * um1 + w2 * u3
        y = y3.reshape(tm, conv)
        cs_out[...] = ubuf[:, 14:16, :]
    else:
        i = pl.program_id(0)

        @pl.when(i % tiles_per_batch == 0)
        def _():
            ubuf[0:8, :] = jnp.zeros((8, conv), F32)

        ubuf[8:tm + 8, :] = u
        um1 = ubuf[7:tm + 7, :]
        um2 = ubuf[6:tm + 6, :]
        y = w0 * um2 + w1 * um1 + w2 * u
        ubuf[0:8, :] = ubuf[tm:tm + 8, :]
        cs_out[...] = u[tm - 2:tm, :]
    yc_out[...] = (cb * y).astype(BF16)


def _premix(x, mod_specs, mods, g1, win_bf, gq, gk, gmat, conv_w, state, *, sample, seq_len, dk):
    t, d = x.shape
    att = gq.shape[1]
    conv = conv_w.shape[1]
    ncol = win_bf.shape[1]
    tm = min(TOKEN_TILE, t)
    nt = t // tm
    tiles_per_batch = max(seq_len // tm, 1)
    q_scale = dk ** -0.5 * LOG2E
    const = lambda shape: pl.BlockSpec(shape, lambda i: (0,) * len(shape))
    row = lambda w: pl.BlockSpec((tm, w), lambda i: (i, 0))
    dv = 2 * dk
    nheads = att // dv
    cache_rows = pl.BlockSpec((tm * nheads, dv), lambda i: (i, 0))
    cache_shape = jax.ShapeDtypeStruct((t * nheads, dv), F32)
    in_specs = [row(d), mod_specs(0, tm), mod_specs(1, tm), const((1, d)), const((d, ncol)),
                const((1, att)), const((1, att)), const((att, att)), const((CONV_WIDTH, conv))]
    args = [x, mods, mods, g1, win_bf, gq, gk, gmat, conv_w]
    if sample:
        nseq = tm // SUBLANES
        in_specs.append(pl.BlockSpec((nseq, CONV_WIDTH - 1, conv), lambda i: (i, 0, 0)))
        args.append(state)
        out_specs = [row(att), cache_rows, cache_rows, row(conv),
                     pl.BlockSpec((nseq, CONV_WIDTH - 1, conv), lambda i: (i, 0, 0))]
        out_shape = [jax.ShapeDtypeStruct((t, att), F32), cache_shape, cache_shape,
                     jax.ShapeDtypeStruct((t, conv), BF16),
                     jax.ShapeDtypeStruct((t // SUBLANES, CONV_WIDTH - 1, conv), F32)]
        scratch = [pltpu.VMEM((nseq, 2 * SUBLANES, conv), F32)]
    else:
        nb = t // seq_len
        out_specs = [row(att), cache_rows, cache_rows, row(att), row(att), row(conv),
                     pl.BlockSpec((None, CONV_WIDTH - 1, conv), lambda i: (i // tiles_per_batch, 0, 0))]
        out_shape = [jax.ShapeDtypeStruct((t, att), BF16), cache_shape, cache_shape,
                     jax.ShapeDtypeStruct((t, att), BF16),
                     jax.ShapeDtypeStruct((t, att), BF16), jax.ShapeDtypeStruct((t, conv), BF16),
                     jax.ShapeDtypeStruct((nb, CONV_WIDTH - 1, conv), F32)]
        scratch = [pltpu.VMEM((tm + SUBLANES, conv), F32)]
    return pl.pallas_call(
        functools.partial(_premix_kernel, sample=sample, tiles_per_batch=tiles_per_batch,
                          att=att, dk=dk, q_scale=q_scale),
        grid=(nt,),
        in_specs=in_specs,
        out_specs=out_specs,
        out_shape=out_shape,
        scratch_shapes=scratch,
        compiler_params=_params("arbitrary"),
        name="premix_sample" if sample else "premix_prompt",
    )(*args)


def _lambda_value(lq1, lk1, lq2, lk2, lam_init):
    a = jnp.sum(lq1[...] * lk1[...], axis=-1, keepdims=True)
    b = jnp.sum(lq2[...] * lk2[...], axis=-1, keepdims=True)
    return jnp.exp(a) - jnp.exp(b) + lam_init


def _attn_kernel(qT_ref, k_ref, vT_ref, lq1, lk1, lq2, lk2, g_ref, o_ref,
                 acc1, acc2, m1, l1, m2, l2, *, dk, lam_init):
    tq = qT_ref.shape[1]
    tk = vT_ref.shape[2]
    qi = pl.program_id(2)
    qT = qT_ref[...].astype(F32)
    half = lax.broadcasted_iota(I32, qT.shape, 0) < dk
    rhs = (jnp.where(half, qT, 0.0).astype(BF16), jnp.where(half, 0.0, qT).astype(BF16))
    accs, ms, ls = (acc1, acc2), (m1, m2), (l1, l2)
    for hf in range(2):
        accs[hf][...] = jnp.zeros(accs[hf].shape, F32)
        ms[hf][...] = jnp.full(ms[hf].shape, -jnp.inf, F32)
        ls[hf][...] = jnp.zeros(ls[hf].shape, F32)

    cw = min(ATTN_COLS, tq)
    units = [(hf, cb) for hf in range(2) for cb in range(tq // cw)]

    def qk(kc, u):
        hf, cb = u
        return jnp.dot(kc, rhs[hf][:, cb * cw:(cb + 1) * cw], preferred_element_type=F32)

    def softmax(s, u, diag):
        hf, cb = u
        cols = slice(cb * cw, (cb + 1) * cw)
        if diag is not None and cb * cw < (diag + 1) * tk - 1:
            kpos = lax.broadcasted_iota(I32, s.shape, 0) + diag * tk
            qpos = lax.broadcasted_iota(I32, s.shape, 1) + cb * cw
            s = jnp.where(kpos <= qpos, s, -jnp.inf)
        m_prev = ms[hf][:, cols]
        m_new = jnp.maximum(m_prev, jnp.max(s, axis=0, keepdims=True))
        alpha = jnp.exp2(m_prev - m_new)
        p = jnp.exp2(s - m_new)
        ls[hf][:, cols] = alpha * ls[hf][:, cols] + jnp.sum(p, axis=0, keepdims=True)
        ms[hf][:, cols] = m_new
        return p.astype(BF16), alpha

    def pv(vc, pa, u):
        hf, cb = u
        cols = slice(cb * cw, (cb + 1) * cw)
        p, alpha = pa
        accs[hf][:, cols] = alpha * accs[hf][:, cols] + jnp.dot(vc, p, preferred_element_type=F32)

    def step(j, diag):
        kc = k_ref[pl.ds(pl.multiple_of(j * tk, tk), tk), :]
        vc = vT_ref[j]
        live = [u for u in units if diag is None or (u[1] + 1) * cw > diag * tk]
        n = len(live)
        s, pa = {}, {}
        lag = ATTN_STAGE_LAG
        for t in range(n + 2 * lag):
            if t < n:
                s[t] = qk(kc, live[t])
            if 0 <= t - lag < n:
                pa[t - lag] = softmax(s.pop(t - lag), live[t - lag], diag)
            if 0 <= t - 2 * lag < n:
                pv(vc, pa.pop(t - 2 * lag), live[t - 2 * lag])

    def body(j, c):
        step(j, None)
        return c

    kpt = tq // tk
    lax.fori_loop(0, qi * kpt, body, 0)
    for c in range(kpt):
        step(qi * kpt + c, c)

    lam = _lambda_value(lq1, lk1, lq2, lk2, lam_init)
    o = acc1[...] / l1[...] - lam * (acc2[...] / l2[...])
    on = o * lax.rsqrt(jnp.mean(o * o, axis=0, keepdims=True) + RMS_EPS) * (1.0 - lam_init)
    o_ref[...] = (on.T * g_ref[...]).astype(o_ref.dtype)


def _attn_prompt(qT, k_bf, vT, lams, subln_g, *, nheads, dk, lam_init):
    b, att, s = qT.shape
    dv = att // nheads
    tq = min(ATTN_TILE, s)
    nq = s // tq
    nk, tk = vT.shape[1] // nheads, vT.shape[3]
    lam_spec = pl.BlockSpec((1, dk), lambda bi, hi, qi: (0, 0))
    return pl.pallas_call(
        functools.partial(_attn_kernel, dk=dk, lam_init=lam_init),
        grid=(b, nheads, nq),
        in_specs=[pl.BlockSpec((None, dv, tq), lambda bi, hi, qi: (bi, hi, qi)),
                  pl.BlockSpec((None, s, dv), lambda bi, hi, qi: (bi, 0, hi)),
                  pl.BlockSpec((None, nk, dv, tk), lambda bi, hi, qi: (bi, hi, 0, 0)),
                  lam_spec, lam_spec, lam_spec, lam_spec,
                  pl.BlockSpec((1, dv), lambda bi, hi, qi: (0, 0))],
        out_specs=pl.BlockSpec((None, tq, dv), lambda bi, hi, qi: (bi, qi, hi)),
        out_shape=jax.ShapeDtypeStruct((b, s, att), BF16),
        scratch_shapes=[pltpu.VMEM((dv, tq), F32), pltpu.VMEM((dv, tq), F32),
                        pltpu.VMEM((1, tq), F32), pltpu.VMEM((1, tq), F32),
                        pltpu.VMEM((1, tq), F32), pltpu.VMEM((1, tq), F32)],
        compiler_params=_params("parallel", "parallel", "arbitrary"),
        name="attn_prompt",
    )(qT, k_bf, vT, *lams, subln_g)


def _sattn_kernel(pt_ref, *rest, pps, nheads, dk, lam_init, carry_dispatch):
    if carry_dispatch:
        dest_ref, last_ref, padded_ref, nu_ref, q_ref, kn_ref, vn_ref, h_ref = rest[:8]
        rest = rest[8:]
    else:
        q_ref, kn_ref, vn_ref = rest[:3]
        rest = rest[3:]
    k_refs = rest[:pps]
    v_refs = rest[pps:2 * pps]
    if carry_dispatch:
        lq1, lk1, lq2, lk2, g_ref, o_ref, xs_hbm, m_sc, l_sc, acc_sc, zbuf, sem = rest[2 * pps:]
    else:
        lq1, lk1, lq2, lk2, g_ref, o_ref, m_sc, l_sc, acc_sc = rest[2 * pps:]
    j = pl.program_id(1)
    if carry_dispatch:
        step = pl.program_id(0) * pl.num_programs(1) + j
        tps = h_ref.shape[0]

        @pl.when(step == 0)
        def _():
            _zero_unrouted_blocks(last_ref, padded_ref, nu_ref, xs_hbm, zbuf, sem.at[1])

        def start_rows(t0, t1):
            for t in range(t0, min(t1, tps)):
                for k in range(TOP_K):
                    d = dest_ref[(step * tps + t) * TOP_K + k]
                    pltpu.make_async_copy(h_ref.at[pl.ds(t, 1)], xs_hbm.at[pl.ds(d, 1)],
                                          sem.at[0]).start(priority=k % 2)
    else:
        def start_rows(t0, t1):
            pass
    nt = q_ref.shape[0]
    dv = 2 * dk
    prow = k_refs[0].shape[0]
    rows = 2 * nheads * nt
    nt_dims = (((1,), (1,)), ((), ()))
    head_shift = (2 * nt).bit_length() - 1

    q8 = q_ref[...]
    first = lax.broadcasted_iota(I32, (nt, dv), 1) < dk
    pieces = []
    for h in range(nheads):
        qh = q8[:, h * dv:(h + 1) * dv]
        pieces += [jnp.where(first, qh, 0.0), jnp.where(first, 0.0, qh)]
    qall = jnp.concatenate(pieces, axis=0).astype(BF16)

    @pl.when(j == 0)
    def _():
        m_sc[...] = jnp.full(m_sc.shape, -jnp.inf, F32)
        l_sc[...] = jnp.zeros(l_sc.shape, F32)
        acc_sc[...] = jnp.zeros(acc_sc.shape, F32)

    def soft(s):
        m = jnp.max(s, axis=1, keepdims=True)
        p = jnp.exp2(s - m).astype(BF16)
        return m, jnp.sum(p.astype(F32), axis=1, keepdims=True), p

    def weigh(p, vals, width):
        pv = None
        for i, v in enumerate(vals):
            d = jnp.dot(p[:, i * width:(i + 1) * width], v, preferred_element_type=F32)
            pv = d if pv is None else pv + d
        return pv

    def merge(parts):
        m = parts[0][0]
        for part in parts[1:]:
            m = jnp.maximum(m, part[0])
        l = acc = None
        for m_p, l_p, acc_p in parts:
            w = jnp.exp2(m_p - m)
            l = w * l_p if l is None else l + w * l_p
            acc = w * acc_p if acc is None else acc + w * acc_p
        return m, l, acc

    row_head = lax.broadcasted_iota(I32, (rows, prow), 0) >> head_shift
    col_head = lax.broadcasted_iota(I32, (rows, prow), 1) & (nheads - 1)
    own = row_head == col_head
    groups = [range(g0, min(g0 + SAMPLE_PAGE_GROUP, pps)) for g0 in range(0, pps, SAMPLE_PAGE_GROUP)]
    parts = [(m_sc[...], l_sc[...], acc_sc[...])]
    n, lag = len(groups), SAMPLE_STAGE_LAG
    s_of, soft_of = {}, {}
    rows_per_stage = -(-h_ref.shape[0] // (n + 2 * lag)) if carry_dispatch else 0
    for t in range(n + 2 * lag):
        start_rows(t * rows_per_stage, (t + 1) * rows_per_stage)
        if t < n:
            s_of[t] = jnp.concatenate(
                [jnp.where(own, lax.dot_general(qall, k_refs[i][...].astype(BF16), nt_dims,
                                                preferred_element_type=F32), -jnp.inf)
                 for i in groups[t]], axis=1)
        if 0 <= t - lag < n:
            soft_of[t - lag] = soft(s_of.pop(t - lag))
        if 0 <= t - 2 * lag < n:
            m_g, l_g, p_g = soft_of.pop(t - 2 * lag)
            parts.append((m_g, l_g, weigh(p_g, [v_refs[i][...].astype(BF16) for i in groups[t - 2 * lag]],
                                          prow)))
    state = merge(parts)
    m_sc[...], l_sc[...], acc_sc[...] = state

    @pl.when(j == pl.num_programs(1) - 1)
    def _():
        nnew = nheads * nt
        zpad = jnp.zeros((LANES - nnew, dv), F32)
        kn = jnp.concatenate([kn_ref[...], zpad], axis=0).astype(BF16)
        vn = jnp.concatenate([vn_ref[...], zpad], axis=0).astype(BF16)
        sn = lax.dot_general(qall, kn, nt_dims, preferred_element_type=F32)
        r = lax.broadcasted_iota(I32, sn.shape, 0)
        c = lax.broadcasted_iota(I32, sn.shape, 1)
        vis = jnp.logical_and((r >> head_shift) == (c & (nheads - 1)),
                              (c >> (nheads.bit_length() - 1)) <= (r & (nt - 1)))
        vis = jnp.logical_and(vis, c < nnew)
        m_n, l_n, p_n = soft(jnp.where(vis, sn, -jnp.inf))
        _, l_fin, acc_fin = merge([state, (m_n, l_n, weigh(p_n, [vn], LANES))])

        lam = _lambda_value(lq1, lk1, lq2, lk2, lam_init)
        o = acc_fin / l_fin
        for h in range(nheads):
            r0 = 2 * h * nt
            d = o[r0:r0 + nt, :] - lam * o[r0 + nt:r0 + 2 * nt, :]
            dn = d * lax.rsqrt(jnp.mean(d * d, axis=-1, keepdims=True) + RMS_EPS)
            o_ref[:, h * dv:(h + 1) * dv] = dn * g_ref[...] * (1.0 - lam_init)

    if carry_dispatch:
        for k in range(TOP_K):
            pltpu.make_async_copy(h_ref, xs_hbm.at[pl.ds(0, h_ref.shape[0])], sem.at[0]).wait()


def _attn_sample(q, k_new, v_new, cache_k, cache_v, page_table, lams, subln_g, *,
                 nheads, dk, dec_seq, lam_init, dispatch=None):
    t, att = q.shape
    nseq = t // dec_seq
    dv = 2 * dk
    npool, page = cache_k.shape[0], cache_k.shape[1]
    npages = page_table.shape[1]
    pps = min(PAGES_PER_STEP, npages)
    nsteps = npages // pps
    prow = page * nheads
    assert nheads & (nheads - 1) == 0 and nheads * dec_seq <= LANES
    ck = cache_k.reshape(npool, prow, dv)
    cv = cache_v.reshape(npool, prow, dv)
    tok_spec = pl.BlockSpec((dec_seq, att), lambda s, j, *_: (s, 0))
    new_spec = pl.BlockSpec((dec_seq * nheads, dv), lambda s, j, *_: (s, 0))
    lam_spec = pl.BlockSpec((1, dk), lambda s, j, *_: (0, 0))

    def page_spec(i):
        return pl.BlockSpec((None, prow, dv),
                            lambda s, j, pt, *_: (pt[s * npages + j * pps + i], 0, 0))

    rows = 2 * nheads * dec_seq
    scalars = [page_table.reshape(-1)]
    tensors = [q, k_new, v_new]
    in_specs = [tok_spec, new_spec, new_spec]
    out_specs = [tok_spec]
    out_shape = [jax.ShapeDtypeStruct((t, att), F32)]
    scratch = [pltpu.VMEM((rows, 1), F32), pltpu.VMEM((rows, 1), F32), pltpu.VMEM((rows, dv), F32)]
    if dispatch is not None:
        dest, last_blk, padded, n_used, h2, n_rows = dispatch
        th, d = h2.shape
        tps = th // (nseq * nsteps)
        assert tps * nseq * nsteps == th and tps % SUBLANES == 0
        scalars += [dest.T.reshape(-1), last_blk, padded, n_used]
        tensors.append(h2)
        in_specs.append(pl.BlockSpec((tps, d), lambda s, j, *_: (s * nsteps + j, 0)))
        out_specs.append(pl.BlockSpec(memory_space=pl.ANY))
        out_shape.append(jax.ShapeDtypeStruct((n_rows, d), h2.dtype))
        scratch += [pltpu.VMEM((EXPERT_BLOCK, d), h2.dtype), pltpu.SemaphoreType.DMA((2,))]
    grid_spec = pltpu.PrefetchScalarGridSpec(
        num_scalar_prefetch=len(scalars),
        grid=(nseq, nsteps),
        in_specs=in_specs + [page_spec(i) for i in range(pps)] + [page_spec(i) for i in range(pps)]
                 + [lam_spec] * 4 + [pl.BlockSpec((1, dv), lambda s, j, *_: (0, 0))],
        out_specs=out_specs,
        scratch_shapes=scratch,
    )
    outs = pl.pallas_call(
        functools.partial(_sattn_kernel, pps=pps, nheads=nheads, dk=dk, lam_init=lam_init,
                          carry_dispatch=dispatch is not None),
        grid_spec=grid_spec,
        out_shape=out_shape,
        compiler_params=_params("arbitrary", "arbitrary"),
        name="attn_sample",
    )(*scalars, *tensors, *([ck] * pps), *([cv] * pps), *lams, subln_g)
    return outs if dispatch is not None else outs[0]


def _post_kernel(x_ref, a_ref, yc_ref, gate_ref, shift_ref, scale_ref, woa_ref, woc_ref, g2_ref,
                 wr_ref, br_ref, tri_ref, x1_out, h2_out, ri_out, ptok_out, cnt_out, carry_sc, *, ne):
    i = pl.program_id(0)
    tm = x_ref.shape[0]

    @pl.when(i == 0)
    def _():
        carry_sc[...] = jnp.zeros(carry_sc.shape, F32)

    mix = (jnp.dot(a_ref[...].astype(BF16), woa_ref[...], preferred_element_type=F32)
           + jnp.dot(yc_ref[...], woc_ref[...], preferred_element_type=F32))
    x1 = x_ref[...] + gate_ref[...] * mix
    ms = jnp.mean(x1 * x1, axis=-1, keepdims=True)
    h2 = (x1 * lax.rsqrt(ms + RMS_EPS) * g2_ref[...]) * (1.0 + scale_ref[...]) + shift_ref[...]
    x1_out[...] = x1
    h2_out[...] = h2

    logits = _dot3(h2, wr_ref[...]) + br_ref[...]
    lt = logits.T[0:ne, :]
    eidx = lax.broadcasted_iota(I32, lt.shape, 0)
    vals, idxs = [], []
    for _ in range(TOP_K):
        mx = jnp.max(lt, axis=0, keepdims=True)
        ik = jnp.min(jnp.where(lt == mx, eidx, ne), axis=0, keepdims=True)
        vals.append(mx)
        idxs.append(ik)
        lt = jnp.where(eidx == ik, -jnp.inf, lt)
    ex = [jnp.exp(v - vals[0]) for v in vals]
    den = ex[0] + ex[1] + ex[2] + ex[3]
    probs = [e / den for e in ex]

    sel = jnp.zeros(lt.shape, F32)
    for ik in idxs:
        sel = sel + jnp.where(eidx == ik, 1.0, 0.0)
    cum = jnp.dot(sel.astype(BF16), tri_ref[...], preferred_element_type=F32) + carry_sc[...]
    ranks = [jnp.sum(jnp.where(eidx == ik, cum, 0.0), axis=0, keepdims=True) for ik in idxs]
    carry_new = carry_sc[...] + jnp.sum(sel, axis=1, keepdims=True)
    carry_sc[...] = carry_new
    cnt_out[...] = jnp.broadcast_to(carry_new, cnt_out.shape)

    ri_out[...] = jnp.concatenate(idxs + [r.astype(I32) for r in ranks], axis=0)
    ppad = jnp.concatenate(probs + [jnp.zeros((LANES - TOP_K, tm), F32)], axis=0)
    ptok_out[...] = ppad.T


def _post(x, a, yc, mod_specs, mods, woa, woc, g2, wr_pad, br_pad, *, seq_len, ne):
    t, d = x.shape
    att = a.shape[1]
    conv = yc.shape[1]
    tm = min(TOKEN_TILE, t)
    nt = t // tm
    tri = (lax.broadcasted_iota(I32, (tm, tm), 0) < lax.broadcasted_iota(I32, (tm, tm), 1)).astype(BF16)
    const = lambda shape: pl.BlockSpec(shape, lambda i: (0,) * len(shape))
    row = lambda w: pl.BlockSpec((tm, w), lambda i: (i, 0))
    return pl.pallas_call(
        functools.partial(_post_kernel, ne=ne),
        grid=(nt,),
        in_specs=[row(d), row(att), row(conv), mod_specs(2, tm), mod_specs(3, tm), mod_specs(4, tm),
                  const((att, d)), const((conv, d)), const((1, d)), const((d, LANES)), const((1, LANES)),
                  const((tm, tm))],
        out_specs=[row(d), row(d), pl.BlockSpec((2 * TOP_K, tm), lambda i: (0, i)), row(LANES),
                   const((ne, LANES))],
        out_shape=[jax.ShapeDtypeStruct((t, d), F32), jax.ShapeDtypeStruct((t, d), F32),
                   jax.ShapeDtypeStruct((2 * TOP_K, t), I32), jax.ShapeDtypeStruct((t, LANES), F32),
                   jax.ShapeDtypeStruct((ne, LANES), F32)],
        scratch_shapes=[pltpu.VMEM((ne, 1), F32)],
        compiler_params=_params("arbitrary"),
        name="post_mix",
    )(x, a, yc, mods, mods, mods, woa, woc, g2, wr_pad, br_pad, tri)


def _zero_unrouted_blocks(last_ref, padded_ref, nu_ref, xs_hbm, zbuf, sem):
    bm = zbuf.shape[0]
    ne = last_ref.shape[0]
    nblk = xs_hbm.shape[0] // bm
    zbuf[...] = jnp.zeros(zbuf.shape, zbuf.dtype)

    def zero_block(b):
        return pltpu.make_async_copy(zbuf, xs_hbm.at[pl.ds(pl.multiple_of(b * bm, bm), bm)], sem)

    def tail_start(b, c):
        zero_block(b).start()
        return c

    def tail_wait(b, c):
        zero_block(b).wait()
        return c

    for e in range(ne):
        @pl.when(padded_ref[e] > 0)
        def _():
            zero_block(last_ref[e]).start()
    lax.fori_loop(nu_ref[0], nblk, tail_start, 0)
    for e in range(ne):
        @pl.when(padded_ref[e] > 0)
        def _():
            zero_block(last_ref[e]).wait()
    lax.fori_loop(nu_ref[0], nblk, tail_wait, 0)


def _dispatch_kernel(last_ref, padded_ref, nu_ref, dest_hbm, h_ref, xs_hbm, idx_smem, zbuf, sem, *, tmd):
    i = pl.program_id(0)

    @pl.when(i == 0)
    def _():
        _zero_unrouted_blocks(last_ref, padded_ref, nu_ref, xs_hbm, zbuf, sem.at[2])

    cp = pltpu.make_async_copy(dest_hbm.at[i], idx_smem, sem.at[0])
    cp.start()
    cp.wait()

    def issue(g, c):
        for u in range(ROW_UNROLL):
            t = g * ROW_UNROLL + u
            for k in range(TOP_K):
                d = idx_smem[k * tmd + t]
                pltpu.make_async_copy(h_ref.at[pl.ds(t, 1)], xs_hbm.at[pl.ds(d, 1)],
                                      sem.at[1]).start(priority=k % 2)
        return c

    lax.fori_loop(0, tmd // ROW_UNROLL, issue, 0)
    for k in range(TOP_K):
        pltpu.make_async_copy(h_ref, xs_hbm.at[pl.ds(0, tmd)], sem.at[1]).wait()


def _dispatch(last_blk, padded, n_used, dest_tiles, h2, n_rows):
    t, d = h2.shape
    nt, width = dest_tiles.shape
    tmd = width // TOP_K
    grid_spec = pltpu.PrefetchScalarGridSpec(
        num_scalar_prefetch=3,
        grid=(nt,),
        in_specs=[pl.BlockSpec(memory_space=pl.ANY), pl.BlockSpec((tmd, d), lambda i, *_: (i, 0))],
        out_specs=pl.BlockSpec(memory_space=pl.ANY),
        scratch_shapes=[pltpu.SMEM((width,), I32), pltpu.VMEM((EXPERT_BLOCK, d), h2.dtype),
                        pltpu.SemaphoreType.DMA((3,))],
    )
    return pl.pallas_call(
        functools.partial(_dispatch_kernel, tmd=tmd),
        grid_spec=grid_spec,
        out_shape=jax.ShapeDtypeStruct((n_rows, d), h2.dtype),
        compiler_params=_params("arbitrary"),
        name="moe_dispatch",
    )(last_blk, padded, n_used, dest_tiles, h2)


def _expert_kernel(be_ref, nu_ref, x_ref, wg_ref, bg_ref, wu_ref, bu_ref, wd_ref, bd_ref, y_ref,
                   wg_sc, wu_sc, wd_sc):
    i = pl.program_id(0)
    used = i < nu_ref[0]
    prev = be_ref[jnp.maximum(i - 1, 0)]
    fresh = jnp.logical_or(i == 0, be_ref[i] != prev)

    @pl.when(jnp.logical_and(used, fresh))
    def _():
        wg_sc[...] = wg_ref[...].astype(BF16)
        wu_sc[...] = wu_ref[...].astype(BF16)
        wd_sc[...] = wd_ref[...].astype(BF16)

    @pl.when(used)
    def _():
        x = x_ref[...].astype(BF16)
        g = jnp.minimum(jnp.dot(x, wg_sc[...], preferred_element_type=F32) + bg_ref[...], SWIGLU_LIMIT)
        u = jnp.clip(jnp.dot(x, wu_sc[...], preferred_element_type=F32) + bu_ref[...],
                     -SWIGLU_LIMIT, SWIGLU_LIMIT)
        a = g * jax.nn.sigmoid(SWIGLU_ALPHA * g) * (u + 1.0)
        y_ref[...] = jnp.dot(a.astype(BF16), wd_sc[...], preferred_element_type=F32) + bd_ref[...]

    @pl.when(jnp.logical_not(used))
    def _():
        y_ref[...] = jnp.zeros(y_ref.shape, F32)


def _experts(block_e, n_used, xs, w_gate, b_gate, w_up, b_up, w_down, b_down):
    n_rows, d = xs.shape
    ne, _, dff = w_gate.shape
    bm = EXPERT_BLOCK
    nblk = n_rows // bm

    def blk(i, be, nu):
        return jnp.minimum(i, nu[0] - 1)

    row_spec = pl.BlockSpec((bm, d), lambda i, be, nu: (blk(i, be, nu), 0))
    w_spec = lambda k, n: pl.BlockSpec((None, k, n), lambda i, be, nu: (be[blk(i, be, nu)], 0, 0))
    grid_spec = pltpu.PrefetchScalarGridSpec(
        num_scalar_prefetch=2,
        grid=(nblk,),
        in_specs=[row_spec, w_spec(d, dff), w_spec(1, dff), w_spec(d, dff), w_spec(1, dff),
                  w_spec(dff, d), w_spec(1, d)],
        out_specs=pl.BlockSpec((bm, d), lambda i, be, nu: (i, 0)),
        scratch_shapes=[pltpu.VMEM((d, dff), BF16), pltpu.VMEM((d, dff), BF16), pltpu.VMEM((dff, d), BF16)],
    )
    return pl.pallas_call(
        _expert_kernel,
        grid_spec=grid_spec,
        out_shape=jax.ShapeDtypeStruct((n_rows, d), F32),
        compiler_params=_params("arbitrary"),
        name="moe_experts",
    )(block_e, n_used, xs, w_gate, b_gate.reshape(ne, 1, dff), w_up, b_up.reshape(ne, 1, dff),
      w_down, b_down.reshape(ne, 1, d))


def _combine_kernel(dest_hbm, x1_ref, gate_ref, p_ref, yb_hbm, o_ref, idx_smem, buf, sem, *, tmc):
    i = pl.program_id(0)
    cp = pltpu.make_async_copy(dest_hbm.at[i], idx_smem, sem.at[0])
    cp.start()
    cp.wait()

    def issue(g, c):
        for u in range(ROW_UNROLL):
            t = g * ROW_UNROLL + u
            for k in range(TOP_K):
                d = idx_smem[k * tmc + t]
                pltpu.make_async_copy(yb_hbm.at[pl.ds(d, 1)], buf.at[k, pl.ds(t, 1)],
                                      sem.at[1]).start(priority=k % 2)
        return c

    lax.fori_loop(0, tmc // ROW_UNROLL, issue, 0)
    for k in range(TOP_K):
        pltpu.make_async_copy(yb_hbm.at[pl.ds(0, tmc)], buf.at[k], sem.at[1]).wait()

    p = p_ref[...]
    y = p[:, 0:1] * buf[0]
    for k in range(1, TOP_K):
        y = y + p[:, k:k + 1] * buf[k]
    o_ref[...] = x1_ref[...] + gate_ref[...] * y


def _combine(dest_tiles, x1, mod_specs, mods, ptok, yb):
    t, d = x1.shape
    nt, width = dest_tiles.shape
    tmc = width // TOP_K
    row = lambda w: pl.BlockSpec((tmc, w), lambda i: (i, 0))
    return pl.pallas_call(
        functools.partial(_combine_kernel, tmc=tmc),
        grid=(nt,),
        in_specs=[pl.BlockSpec(memory_space=pl.ANY), row(d), mod_specs(5, tmc), row(LANES),
                  pl.BlockSpec(memory_space=pl.ANY)],
        out_specs=row(d),
        out_shape=jax.ShapeDtypeStruct((t, d), F32),
        scratch_shapes=[pltpu.SMEM((width,), I32), pltpu.VMEM((TOP_K, tmc, d), F32),
                        pltpu.SemaphoreType.DMA((2,))],
        compiler_params=_params("arbitrary"),
        name="moe_combine",
    )(dest_tiles, x1, mods, ptok, yb)


def _dest_tiles(dest, tile):
    t = dest.shape[1]
    return dest.reshape(TOP_K, t // tile, tile).transpose(1, 0, 2).reshape(t // tile, TOP_K * tile)


def _moe_plan(ri, cnt, t, ne):
    bm = EXPERT_BLOCK
    n_rows = -(-(t * TOP_K) // bm) * bm + ne * bm
    counts = cnt[:, 0].astype(I32)
    padded = (counts + bm - 1) // bm * bm
    pend = jnp.cumsum(padded)
    pstart = pend - padded
    eids = jnp.arange(ne, dtype=I32)
    dest = ri[TOP_K:2 * TOP_K] + jnp.sum(
        jnp.where(ri[0:TOP_K, :, None] == eids, pstart, 0), axis=-1)
    nblk = n_rows // bm
    first_row = jnp.arange(nblk, dtype=I32) * bm
    block_e = jnp.minimum(jnp.sum((pend[None, :] <= first_row[:, None]).astype(I32), axis=1), ne - 1)
    return dict(dest=dest, n_rows=n_rows, block_e=block_e, n_used=(pend[-1:] // bm).astype(I32),
                last_blk=(pend // bm - 1).astype(I32), padded=padded.astype(I32))


def _moe_dispatch(plan, h2):
    tmd = min(TOKEN_TILE, h2.shape[0])
    return _dispatch(plan["last_blk"], plan["padded"], plan["n_used"], _dest_tiles(plan["dest"], tmd),
                     h2, plan["n_rows"])


def _moe_finish(plan, xs, x1, ptok, mod_specs, mods, w_gate, b_gate, w_up, b_up, w_down, b_down):
    tmc = min(COMBINE_TILE, x1.shape[0])
    yb = _experts(plan["block_e"], plan["n_used"], xs, w_gate, b_gate, w_up, b_up, w_down, b_down)
    return _combine(_dest_tiles(plan["dest"], tmc), x1, mod_specs, mods, ptok, yb)


def kernel(x_prompt, x_sample, cache_k, cache_v, state_conv, page_table, c_prompt, c_sample,
           w_ada, b_ada, norm1_g, w_in, q_norm_g, k_norm_g, lambda_q1, lambda_k1, lambda_q2,
           lambda_k2, subln_g, conv_w, w_out, norm2_g, w_router, b_router, w_gate, b_gate,
           w_up, b_up, w_down, b_down):
    nb, seq, d = x_prompt.shape
    db, dec_seq, _ = x_sample.shape
    depth, _, page, nheads, dv = cache_k.shape
    dk = dv // 2
    att = nheads * dv
    conv = conv_w.shape[2]
    ne = w_router.shape[2]
    tp, ts = nb * seq, db * dec_seq
    assert dec_seq == SUBLANES and att % LANES == 0 and conv % LANES == 0 and ne <= LANES
    assert dk & (dk - 1) == 0

    c_rows = -(-(nb + db) // SUBLANES) * SUBLANES
    c_all = jnp.concatenate([c_prompt, c_sample, jnp.zeros((c_rows - nb - db, d), F32)], axis=0)
    gidx = jnp.arange(att, dtype=I32) // dk
    gmat = (gidx[:, None] == gidx[None, :]).astype(BF16)

    yp = x_prompt.reshape(tp, d)
    ys = x_sample.reshape(ts, d)
    outs = [[] for _ in range(6)]
    for l in range(depth):
        lam_init = 0.8 - 0.6 * math.exp(-0.3 * l)
        mod_all = _adaln(c_all, w_ada[l], b_ada[l])
        mod_p = mod_all[:nb].reshape(nb, N_ADA, 1, d)
        mod_s = jnp.repeat(mod_all[nb:nb + db], dec_seq, axis=0)

        def mod_specs_p(j, tile):
            per_batch = max(seq // tile, 1)
            return pl.BlockSpec((None, None, 1, d), lambda i: (i // per_batch, j, 0, 0))

        def mod_specs_s(j, tile):
            return pl.BlockSpec((tile, d), lambda i: (i, j))

        win_bf = w_in[l].astype(BF16)
        woa = w_out[l][:att].astype(BF16)
        woc = w_out[l][att:].astype(BF16)
        g1 = norm1_g[l].reshape(1, d)
        g2 = norm2_g[l].reshape(1, d)
        gq = jnp.tile(q_norm_g[l], att // dk).reshape(1, att)
        gk = jnp.tile(k_norm_g[l], att // dk).reshape(1, att)
        lams = [v[l].reshape(1, dk) for v in (lambda_q1, lambda_k1, lambda_q2, lambda_k2)]
        sg = subln_g[l].reshape(1, dv)
        wr_pad = jnp.pad(w_router[l], ((0, 0), (0, LANES - ne)))
        br_pad = jnp.pad(b_router[l], (0, LANES - ne)).reshape(1, LANES)
        moe_w = (w_gate[l], b_gate[l], w_up[l], b_up[l], w_down[l], b_down[l])

        q_bf, k_f, v_f, k_bf, v_bf, yc, conv_p = _premix(
            yp, mod_specs_p, mod_p, g1, win_bf, gq, gk, gmat, conv_w[l], None,
            sample=False, seq_len=seq, dk=dk)
        tk = min(ATTN_KEYS, seq)
        nk = seq // tk
        qT = q_bf.reshape(nb, seq, att).transpose(0, 2, 1)
        vT = (v_bf.reshape(nb, nk, tk, nheads, dv).transpose(0, 3, 1, 4, 2)
              .reshape(nb, nheads * nk, dv, tk))
        a_p = _attn_prompt(qT, k_bf.reshape(nb, seq, att), vT, lams, sg,
                           nheads=nheads, dk=dk, lam_init=lam_init).reshape(tp, att)
        x1, h2, ri, ptok, cnt = _post(yp, a_p, yc, mod_specs_p, mod_p, woa, woc, g2, wr_pad, br_pad,
                                      seq_len=seq, ne=ne)
        plan_p = _moe_plan(ri, cnt, tp, ne)
        outs[0].append(k_f.reshape(nb, seq, nheads, dv))
        outs[1].append(v_f.reshape(nb, seq, nheads, dv))
        outs[2].append(conv_p)

        q_s, k_s, v_s, yc_s, conv_s = _premix(
            ys, mod_specs_s, mod_s, g1, win_bf, gq, gk, gmat, conv_w[l], state_conv[l],
            sample=True, seq_len=dec_seq, dk=dk)
        npages = page_table.shape[1]
        attn_steps = db * (npages // min(PAGES_PER_STEP, npages))
        carry = tp % (attn_steps * SUBLANES) == 0 and tp * TOP_K <= MAX_SMEM_DEST
        sattn = functools.partial(_attn_sample, q_s, k_s, v_s, cache_k[l], cache_v[l], page_table, lams, sg,
                                  nheads=nheads, dk=dk, dec_seq=dec_seq, lam_init=lam_init)
        if carry:
            a_s, xs_p = sattn(dispatch=(plan_p["dest"], plan_p["last_blk"], plan_p["padded"],
                                        plan_p["n_used"], h2, plan_p["n_rows"]))
        else:
            a_s, xs_p = sattn(), _moe_dispatch(plan_p, h2)
        yp = _moe_finish(plan_p, xs_p, x1, ptok, mod_specs_p, mod_p, *moe_w)

        x1, h2, ri, ptok, cnt = _post(ys, a_s, yc_s, mod_specs_s, mod_s, woa, woc, g2, wr_pad, br_pad,
                                      seq_len=dec_seq, ne=ne)
        plan_s = _moe_plan(ri, cnt, ts, ne)
        ys = _moe_finish(plan_s, _moe_dispatch(plan_s, h2), x1, ptok, mod_specs_s, mod_s, *moe_w)
        outs[3].append(k_s.reshape(db, dec_seq, nheads, dv))
        outs[4].append(v_s.reshape(db, dec_seq, nheads, dv))
        outs[5].append(conv_s)

    return (yp.reshape(nb, seq, d), ys.reshape(db, dec_seq, d),
            jnp.stack(outs[0], 0), jnp.stack(outs[1], 0), jnp.stack(outs[2], 0),
            jnp.stack(outs[3], 0), jnp.stack(outs[4], 0), jnp.stack(outs[5], 0))
```
